```python
import jax, jax.numpy as jnp
from jax import lax
import numpy as np

D_MODEL = 4096
BATCH = 8
SEQ = 2048
DEPTH = 4
DEC_BATCH = 4
DEC_SEQ = 4096
PAST_LEN = 128

N_EVEN = (DEPTH + 1) // 2
N_ODD = DEPTH // 2
D_FF = 3 * D_MODEL // 8
EPS = 1e-6
D_MIX = D_MODEL // 2
MIX_HALF = D_MIX // 2

A_HEAD_DIM = 128
A_HEADS = MIX_HALF // A_HEAD_DIM
A_KV_HEADS = 2
A_WINDOW = 128
A_BLOCK = 128
A_Q = A_HEADS * A_HEAD_DIM
A_KV = A_KV_HEADS * A_HEAD_DIM

B_CH = MIX_HALF
B_CONV = 31
EVEN_IN = A_Q + 2 * A_KV + 2 * B_CH
EVEN_OUT = A_Q + B_CH

C_HEADS = 4
C_DV = MIX_HALF // C_HEADS
C_DK = C_DV // 2
C_RANK = 16
C_TAU = 16.0
C_CHUNK = 16
C_QK = C_HEADS * C_DK
C_V = C_HEADS * C_DV

D_WIDTH = MIX_HALF
D_BLOCKS = 8
D_BLOCK_DIM = D_WIDTH // D_BLOCKS
D_CONV = 4
D_C = 8.0
ODD_IN = 2 * C_QK + 2 * C_V + 2 * C_RANK + 2 * D_WIDTH
ODD_OUT = C_V + D_WIDTH

kernel_name = "hybrid_bidir_encoder_trunk"


def rmsnorm(x, g):
    xf = x.astype(jnp.float32)
    inv = lax.rsqrt(jnp.mean(xf * xf, axis=-1, keepdims=True) + EPS)
    return (xf * inv).astype(x.dtype) * g


def layernorm(x, g, b):
    xf = x.astype(jnp.float32)
    mu = jnp.mean(xf, axis=-1, keepdims=True)
    xc = xf - mu
    var = jnp.mean(xc * xc, axis=-1, keepdims=True)
    return (xc * lax.rsqrt(var + EPS)).astype(x.dtype) * g + b


def swiglu_ffn(x, w_in, w_out):
    gate, up = jnp.split(x @ w_in, 2, axis=-1)
    return (jax.nn.silu(gate) * up) @ w_out


def depthwise_conv_centred(x, w, b):
    width = w.shape[0]
    left = (width - 1) // 2
    y = lax.conv_general_dilated(
        x, w[:, None, :], window_strides=(1,), padding=[(left, width - 1 - left)],
        dimension_numbers=("NWC", "WIO", "NWC"), feature_group_count=x.shape[-1])
    return y + b


def windowed_attention(q, k, v, sink):
    bsz, seq, _, dh = q.shape
    g = A_HEADS // A_KV_HEADS
    nb = seq // A_BLOCK
    qb = q.reshape(bsz, nb, A_BLOCK, A_KV_HEADS, g, dh)
    pad = ((0, 0), (A_BLOCK, A_BLOCK), (0, 0), (0, 0))
    kp = jnp.pad(k, pad).reshape(bsz, nb + 2, A_BLOCK, A_KV_HEADS, dh)
    vp = jnp.pad(v, pad).reshape(bsz, nb + 2, A_BLOCK, A_KV_HEADS, dh)
    kband = jnp.concatenate([kp[:, :-2], kp[:, 1:-1], kp[:, 2:]], axis=2)
    vband = jnp.concatenate([vp[:, :-2], vp[:, 1:-1], vp[:, 2:]], axis=2)
    scores = jnp.einsum("bnqkgd,bnskd->bnkgqs", qb, kband).astype(jnp.float32) * (dh ** -0.5)
    qpos = jnp.arange(A_BLOCK)
    kpos = jnp.arange(3 * A_BLOCK) - A_BLOCK
    rel = kpos[None, :] - qpos[:, None]
    abs_k = (jnp.arange(nb) * A_BLOCK)[:, None] + kpos[None, :]
    valid = (jnp.abs(rel) <= A_WINDOW)[None] & ((abs_k >= 0) & (abs_k < seq))[:, None, :]
    slopes = jnp.exp2(-8.0 * jnp.arange(1, A_HEADS + 1, dtype=jnp.float32) / A_HEADS)
    alibi = -slopes.reshape(A_KV_HEADS, g, 1, 1) * jnp.abs(rel).astype(jnp.float32)
    scores = jnp.where(valid[None, :, None, None], scores + alibi[None, None], -jnp.inf)
    sink_logit = jnp.broadcast_to(sink.astype(jnp.float32).reshape(A_KV_HEADS, g, 1, 1),
                                  (bsz, nb, A_KV_HEADS, g, A_BLOCK, 1))
    probs = jax.nn.softmax(jnp.concatenate([scores, sink_logit], axis=-1), axis=-1)[..., :-1]
    out = jnp.einsum("bnkgqs,bnskd->bnqkgd", probs.astype(v.dtype), vband)
    return out.reshape(bsz, seq, A_HEADS * dh)


def even_mixer(h, w_in, w_out, q_gain, k_gain, sink, conv_w, conv_b, cn_g, cn_b):
    bsz, seq, _ = h.shape
    proj = h @ w_in
    q, k, v, glu = jnp.split(proj, [A_Q, A_Q + A_KV, A_Q + 2 * A_KV], axis=-1)
    q = rmsnorm(q.reshape(bsz, seq, A_HEADS, A_HEAD_DIM), q_gain)
    k = rmsnorm(k.reshape(bsz, seq, A_KV_HEADS, A_HEAD_DIM), k_gain)
    v = v.reshape(bsz, seq, A_KV_HEADS, A_HEAD_DIM)
    a_out = windowed_attention(q, k, v, sink)
    u, gate = jnp.split(glu, 2, axis=-1)
    c = depthwise_conv_centred(u * jax.nn.sigmoid(gate), conv_w, conv_b)
    c = jax.nn.silu(layernorm(c, cn_g, cn_b))
    return jnp.concatenate([a_out, c], axis=-1) @ w_out


def gla_causal(q, k, v, log_a):
    dt = v.dtype
    q, k, v = q.astype(jnp.float32), k.astype(jnp.float32), v.astype(jnp.float32)
    bsz, seq, nh, dk = q.shape
    dv = v.shape[-1]
    L = C_CHUNK
    nc = seq // L
    q = q.reshape(bsz, nc, L, nh, dk)
    k = k.reshape(bsz, nc, L, nh, dk)
    v = v.reshape(bsz, nc, L, nh, dv)
    b = jnp.cumsum(log_a.astype(jnp.float32).reshape(bsz, nc, L, nh, dk), axis=2)
    q_in = q * jnp.exp(b)
    k_in = k * jnp.exp(-b)
    causal = jnp.tril(jnp.ones((L, L), dtype=bool))
    att = jnp.einsum("bnthd,bnshd->bnhts", q_in, k_in)
    att = jnp.where(causal, att, 0.0)
    o_intra = jnp.einsum("bnhts,bnshv->bnthv", att, v)
    b_last = b[:, :, -1]
    k_out = k * jnp.exp(b_last[:, :, None] - b)

    def step(state, xs):
        q_c, k_c, v_c, dec_c = xs
        o_c = jnp.einsum("bthd,bhdv->bthv", q_c, state)
        state = dec_c[..., None] * state + jnp.einsum("bthd,bthv->bhdv", k_c, v_c)
        return state, o_c

    xs = (jnp.moveaxis(q_in, 1, 0), jnp.moveaxis(k_out, 1, 0), jnp.moveaxis(v, 1, 0),
          jnp.moveaxis(jnp.exp(b_last), 1, 0))
    state0 = jnp.zeros((bsz, nh, dk, dv), jnp.float32)
    _, o_inter = lax.scan(step, state0, xs)
    o = o_intra + jnp.moveaxis(o_inter, 0, 1)
    return o.reshape(bsz, seq, nh, dv).astype(dt)


def rg_lru(x, wa, ba, wx, bx, lam, reverse):
    bsz, seq, _ = x.shape
    xb = x.reshape(bsz, seq, D_BLOCKS, D_BLOCK_DIM)
    r = jax.nn.sigmoid(jnp.einsum("bshi,hij->bshj", xb, wa).reshape(bsz, seq, D_WIDTH) + ba)
    i = jax.nn.sigmoid(jnp.einsum("bshi,hij->bshj", xb, wx).reshape(bsz, seq, D_WIDTH) + bx)
    log_a = D_C * r.astype(jnp.float32) * jax.nn.log_sigmoid(lam.astype(jnp.float32))
    a = jnp.exp(log_a)
    u = jnp.sqrt(-jnp.expm1(2.0 * log_a)) * (i * x).astype(jnp.float32)

    def step(h, au):
        a_t, u_t = au
        h = a_t * h + u_t
        return h, h

    h0 = jnp.zeros((bsz, D_WIDTH), jnp.float32)
    _, hs = lax.scan(step, h0, (jnp.moveaxis(a, 1, 0), jnp.moveaxis(u, 1, 0)), reverse=reverse)
    return jnp.moveaxis(hs, 0, 1).astype(x.dtype)


def odd_mixer(h, w_in, w_out, gate_up, gate_bias, c_norm_g, conv_w, conv_b, wa, ba, wx, bx, lam):
    bsz, seq, _ = h.shape
    proj = h @ w_in
    o1 = C_QK
    o2 = o1 + C_QK
    o3 = o2 + C_V
    o4 = o3 + C_V
    o5 = o4 + 2 * C_RANK
    o6 = o5 + D_WIDTH
    q, k, v, og, lr, xr, yg = jnp.split(proj, [o1, o2, o3, o4, o5, o6], axis=-1)
    q = q.reshape(bsz, seq, C_HEADS, C_DK) * (C_DK ** -0.5)
    k = k.reshape(bsz, seq, C_HEADS, C_DK)
    v = v.reshape(bsz, seq, C_HEADS, C_DV)
    lr_f, lr_b = jnp.split(lr, 2, axis=-1)
    loga_f = (jax.nn.log_sigmoid((lr_f @ gate_up[0] + gate_bias[0]).astype(jnp.float32)) / C_TAU
              ).reshape(bsz, seq, C_HEADS, C_DK)
    loga_b = (jax.nn.log_sigmoid((lr_b @ gate_up[1] + gate_bias[1]).astype(jnp.float32)) / C_TAU
              ).reshape(bsz, seq, C_HEADS, C_DK)
    o_f = gla_causal(q, k, v, loga_f)
    o_b = jnp.flip(gla_causal(jnp.flip(q, 1), jnp.flip(k, 1), jnp.flip(v, 1), jnp.flip(loga_b, 1)), 1)
    c_out = rmsnorm(o_f + o_b, c_norm_g).reshape(bsz, seq, C_V) * jax.nn.silu(og)
    xc = depthwise_conv_centred(xr, conv_w, conv_b)
    h_f = rg_lru(xc, wa[0], ba[0], wx[0], bx[0], lam[0], reverse=False)
    h_b = rg_lru(xc, wa[1], ba[1], wx[1], bx[1], lam[1], reverse=True)
    d_out = (h_f + h_b) * jax.nn.gelu(yg)
    return jnp.concatenate([c_out, d_out], axis=-1) @ w_out


def run_trunk(x, params):
    (ln_ffn1, ffn1_w_in, ffn1_w_out, ln_mix, ln_ffn2, ffn2_w_in, ffn2_w_out,
     ev_w_in, ev_w_out, a_q_gain, a_k_gain, a_sink, b_conv_w, b_conv_b, b_norm_g, b_norm_b,
     od_w_in, od_w_out, c_gate_up, c_gate_bias, c_norm_g, d_conv_w, d_conv_b,
     d_wa, d_ba, d_wx, d_bx, d_lambda) = params
    for layer in range(DEPTH):
        x = x + 0.5 * swiglu_ffn(rmsnorm(x, ln_ffn1[layer]), ffn1_w_in[layer], ffn1_w_out[layer])
        h = rmsnorm(x, ln_mix[layer])
        j = layer // 2
        if layer % 2 == 0:
            x = x + even_mixer(h, ev_w_in[j], ev_w_out[j], a_q_gain[j], a_k_gain[j], a_sink[j],
                               b_conv_w[j], b_conv_b[j], b_norm_g[j], b_norm_b[j])
        else:
            x = x + odd_mixer(h, od_w_in[j], od_w_out[j], c_gate_up[j], c_gate_bias[j], c_norm_g[j],
                              d_conv_w[j], d_conv_b[j], d_wa[j], d_ba[j], d_wx[j], d_bx[j], d_lambda[j])
        x = x + 0.5 * swiglu_ffn(rmsnorm(x, ln_ffn2[layer]), ffn2_w_in[layer], ffn2_w_out[layer])
    return x


def setup_inputs(seed: int = 0) -> dict:
    key = jax.random.key(seed)
    ks = jax.random.split(key, 32)
    f32 = jnp.float32

    def normal(k, shape, scale):
        return jax.random.normal(k, shape, f32) * scale

    def gain(k, shape):
        return 1.0 + 0.02 * jax.random.normal(k, shape, f32)

    u = jax.random.uniform(ks[30], (N_ODD, 2, D_WIDTH), f32, minval=0.9, maxval=0.999)
    p = u ** (1.0 / D_C)
    d_lambda = jnp.log(p) - jnp.log1p(-p)
    return {
        "x_prompt": normal(ks[0], (BATCH, SEQ, D_MODEL), 1.0),
        "x_sample": normal(ks[1], (DEC_BATCH, DEC_SEQ, D_MODEL), 1.0),
        "ln_ffn1": gain(ks[2], (DEPTH, D_MODEL)),
        "ffn1_w_in": normal(ks[3], (DEPTH, D_MODEL, 2 * D_FF), D_MODEL ** -0.5),
        "ffn1_w_out": normal(ks[4], (DEPTH, D_FF, D_MODEL), D_FF ** -0.5),
        "ln_mix": gain(ks[5], (DEPTH, D_MODEL)),
        "ln_ffn2": gain(ks[6], (DEPTH, D_MODEL)),
        "ffn2_w_in": normal(ks[7], (DEPTH, D_MODEL, 2 * D_FF), D_MODEL ** -0.5),
        "ffn2_w_out": normal(ks[8], (DEPTH, D_FF, D_MODEL), D_FF ** -0.5),
        "ev_w_in": normal(ks[9], (N_EVEN, D_MODEL, EVEN_IN), D_MODEL ** -0.5),
        "ev_w_out": normal(ks[10], (N_EVEN, EVEN_OUT, D_MODEL), EVEN_OUT ** -0.5),
        "a_q_gain": gain(ks[11], (N_EVEN, A_HEAD_DIM)),
        "a_k_gain": gain(ks[12], (N_EVEN, A_HEAD_DIM)),
        "a_sink": normal(ks[13], (N_EVEN, A_HEADS), 0.5),
        "b_conv_w": normal(ks[14], (N_EVEN, B_CONV, B_CH), B_CONV ** -0.5),
        "b_conv_b": normal(ks[15], (N_EVEN, B_CH), 0.02),
        "b_norm_g": gain(ks[16], (N_EVEN, B_CH)),
        "b_norm_b": normal(ks[17], (N_EVEN, B_CH), 0.02),
        "od_w_in": normal(ks[18], (N_ODD, D_MODEL, ODD_IN), D_MODEL ** -0.5),
        "od_w_out": normal(ks[19], (N_ODD, ODD_OUT, D_MODEL), ODD_OUT ** -0.5),
        "c_gate_up": normal(ks[20], (N_ODD, 2, C_RANK, C_QK), C_RANK ** -0.5),
        "c_gate_bias": normal(ks[21], (N_ODD, 2, C_QK), 0.1),
        "c_norm_g": gain(ks[22], (N_ODD, C_DV)),
        "d_conv_w": normal(ks[23], (N_ODD, D_CONV, D_WIDTH), D_CONV ** -0.5),
        "d_conv_b": normal(ks[24], (N_ODD, D_WIDTH), 0.02),
        "d_wa": normal(ks[25], (N_ODD, 2, D_BLOCKS, D_BLOCK_DIM, D_BLOCK_DIM), D_BLOCK_DIM ** -0.5),
        "d_ba": normal(ks[26], (N_ODD, 2, D_WIDTH), 0.1),
        "d_wx": normal(ks[27], (N_ODD, 2, D_BLOCKS, D_BLOCK_DIM, D_BLOCK_DIM), D_BLOCK_DIM ** -0.5),
        "d_bx": normal(ks[28], (N_ODD, 2, D_WIDTH), 0.1),
        "d_lambda": d_lambda,
    }


def reference(x_prompt, x_sample, ln_ffn1, ffn1_w_in, ffn1_w_out, ln_mix, ln_ffn2, ffn2_w_in, ffn2_w_out,
              ev_w_in, ev_w_out, a_q_gain, a_k_gain, a_sink, b_conv_w, b_conv_b, b_norm_g, b_norm_b,
              od_w_in, od_w_out, c_gate_up, c_gate_bias, c_norm_g, d_conv_w, d_conv_b,
              d_wa, d_ba, d_wx, d_bx, d_lambda):
    params = (ln_ffn1, ffn1_w_in, ffn1_w_out, ln_mix, ln_ffn2, ffn2_w_in, ffn2_w_out,
              ev_w_in, ev_w_out, a_q_gain, a_k_gain, a_sink, b_conv_w, b_conv_b, b_norm_g, b_norm_b,
              od_w_in, od_w_out, c_gate_up, c_gate_bias, c_norm_g, d_conv_w, d_conv_b,
              d_wa, d_ba, d_wx, d_bx, d_lambda)
    y_prompt = run_trunk(x_prompt, params)
    y_sample = run_trunk(x_sample, params)
    return (y_prompt, y_sample)
```

```python
import functools

import jax
import jax.numpy as jnp
from jax import lax
from jax.experimental import pallas as pl
from jax.experimental.pallas import tpu as pltpu

F32 = jnp.float32
BF16 = jnp.bfloat16

D_MODEL = 4096
DEPTH = 4
D_FF = 3 * D_MODEL // 8
EPS = 1e-6
MIX_HALF = D_MODEL // 4
A_HEAD_DIM = 128
A_HEADS = MIX_HALF // A_HEAD_DIM
A_KV_HEADS = 2
A_GROUP = A_HEADS // A_KV_HEADS
A_WINDOW = 128
A_BLOCK = 128
A_Q = A_HEADS * A_HEAD_DIM
A_KV = A_KV_HEADS * A_HEAD_DIM
B_CH = MIX_HALF
B_CONV = 31
C_HEADS = 4
C_DV = MIX_HALF // C_HEADS
C_DK = C_DV // 2
C_RANK = 16
C_TAU = 16.0
C_QK = C_HEADS * C_DK
C_V = C_HEADS * C_DV
D_WIDTH = MIX_HALF
D_BLOCKS = 8
D_BLOCK_DIM = D_WIDTH // D_BLOCKS
D_CONV = 4
D_C = 8.0

V7X_LANES = 128
V7X_SUBLANES = 8
V7X_VMEM_BYTES = 64 * 1024 * 1024

TM = 256
TM_NORM = 512
FF_CHUNK = 512
OUT_CHUNK = 1024
CONV_T = 256
CONV_HALO = 16
CONV_RC = 64
CONV_LC = 256
GLA_T = 256
GLA_C = 64
GLA_SUB = 16
LR_PAD = 128


def _vmem_limit(nbytes):
    return int(min(nbytes * 5 // 4 + (4 << 20), V7X_VMEM_BYTES - (4 << 20)))


def _params(sem, nbytes):
    return pltpu.CompilerParams(dimension_semantics=sem, vmem_limit_bytes=_vmem_limit(nbytes))


def _resident(shape):
    nd = len(shape)
    return pl.BlockSpec(shape, lambda *_: (0,) * nd, pipeline_mode=pl.Buffered(1))


def _rows(tm, width):
    return pl.BlockSpec((tm, width), lambda i: (i, 0))


def _dot(a, b):
    return jnp.dot(a, b, preferred_element_type=F32)


def _dot_nt(a, b):
    return lax.dot_general(a, b, (((1,), (1,)), ((), ())), preferred_element_type=F32)


def _dot_tn(a, b):
    return lax.dot_general(a, b, (((0,), (0,)), ((), ())), preferred_element_type=F32)


def _rms(x, g):
    inv = lax.rsqrt(jnp.mean(x * x, axis=-1, keepdims=True) + EPS)
    return (x * inv) * g


def _log_sigmoid(x):
    return jnp.minimum(x, 0.0) - jnp.log1p(jnp.exp(-jnp.abs(x)))


def _norm_kernel(x_ref, g_ref, o_ref):
    o_ref[...] = _rms(x_ref[...], g_ref[...]).astype(BF16)


def _norm(x, g):
    n, d = x.shape
    return pl.pallas_call(
        _norm_kernel,
        out_shape=jax.ShapeDtypeStruct((n, d), BF16),
        grid=(n // TM_NORM,),
        in_specs=[_rows(TM_NORM, d), _resident((1, d))],
        out_specs=_rows(TM_NORM, d),
        compiler_params=_params(("parallel",), 2 * TM_NORM * d * 6),
        name="rmsnorm",
    )(x, g)


def _ffn_in_kernel(xn_ref, w_ref, act_ref):
    xn = xn_ref[...]
    for c in range(D_FF // FF_CHUNK):
        lo = c * FF_CHUNK
        gate = _dot(xn, w_ref[:, lo:lo + FF_CHUNK])
        up = _dot(xn, w_ref[:, D_FF + lo:D_FF + lo + FF_CHUNK])
        act_ref[:, lo:lo + FF_CHUNK] = (gate * jax.nn.sigmoid(gate) * up).astype(BF16)


def _ffn_in(xn, w_in):
    n, d = xn.shape
    return pl.pallas_call(
        _ffn_in_kernel,
        out_shape=jax.ShapeDtypeStruct((n, D_FF), BF16),
        grid=(n // TM,),
        in_specs=[_rows(TM, d), _resident(w_in.shape)],
        out_specs=_rows(TM, D_FF),
        compiler_params=_params(("parallel",), w_in.size * 2 + 2 * TM * (d + D_FF) * 2 + 4 * TM * FF_CHUNK * 4),
        name="ffn_in",
    )(xn, w_in)


def _residual_out(x_ref, xo_ref, xn_ref, g_ref, scale, branch_fn):
    tm = x_ref.shape[0]
    ssq = jnp.zeros((tm, 1), F32)
    for c in range(D_MODEL // OUT_CHUNK):
        lo = c * OUT_CHUNK
        y = branch_fn(lo, lo + OUT_CHUNK)
        if scale != 1.0:
            y = scale * y
        xnew = x_ref[:, lo:lo + OUT_CHUNK] + y
        xo_ref[:, lo:lo + OUT_CHUNK] = xnew
        ssq = ssq + jnp.sum(xnew * xnew, axis=-1, keepdims=True)
    if xn_ref is not None:
        inv = lax.rsqrt(ssq * (1.0 / D_MODEL) + EPS)
        for c in range(D_MODEL // OUT_CHUNK):
            lo = c * OUT_CHUNK
            xn_ref[:, lo:lo + OUT_CHUNK] = (
                (xo_ref[:, lo:lo + OUT_CHUNK] * inv) * g_ref[:, lo:lo + OUT_CHUNK]).astype(BF16)


def _ffn_out_kernel(act_ref, x_ref, w_ref, g_ref, xo_ref, *maybe_xn):
    xn_ref = maybe_xn[0] if maybe_xn else None
    act = act_ref[...]
    _residual_out(x_ref, xo_ref, xn_ref, g_ref, 0.5, lambda lo, hi: _dot(act, w_ref[:, lo:hi]))


def _out_call(kernel, name, row_inputs, x, weights, g_next, emit_norm):
    n, d = x.shape
    out_shape = [jax.ShapeDtypeStruct((n, d), F32)]
    out_specs = [_rows(TM, d)]
    if emit_norm:
        out_shape.append(jax.ShapeDtypeStruct((n, d), BF16))
        out_specs.append(_rows(TM, d))
    nbytes = sum(w.size * w.dtype.itemsize for w in weights)
    nbytes += 2 * sum(TM * a.shape[1] * a.dtype.itemsize for a in row_inputs)
    nbytes += 2 * TM * d * (4 + 4 + 2) + 2 * TM * OUT_CHUNK * 4
    res = pl.pallas_call(
        kernel,
        out_shape=out_shape,
        grid=(n // TM,),
        in_specs=([_rows(TM, a.shape[1]) for a in row_inputs] + [_rows(TM, d)]
                  + [_resident(w.shape) for w in weights] + [_resident((1, d))]),
        out_specs=out_specs,
        compiler_params=_params(("parallel",), nbytes),
        name=name,
    )(*row_inputs, x, *weights, g_next)
    return (res[0], res[1]) if emit_norm else (res[0], None)


def _ffn_out(act, x, w_out, g_next, emit_norm):
    return _out_call(_ffn_out_kernel, "ffn_out", [act], x, [w_out], g_next, emit_norm)


def _head_norm(p, gain):
    inv = lax.rsqrt(jnp.mean(p * p, axis=-1, keepdims=True) + EPS)
    return ((p * inv) * gain).astype(BF16)


def _even_in_kernel(xn_ref, w_ref, qg_ref, kg_ref, q_ref, k_ref, v_ref, glu_ref):
    xn = xn_ref[...]
    hd = A_HEAD_DIM
    for c in range(A_Q // FF_CHUNK):
        p = _dot(xn, w_ref[:, c * FF_CHUNK:(c + 1) * FF_CHUNK])
        for h in range(FF_CHUNK // hd):
            q_ref[:, c * FF_CHUNK + h * hd:c * FF_CHUNK + (h + 1) * hd] = _head_norm(
                p[:, h * hd:(h + 1) * hd], qg_ref[...])
    p = _dot(xn, w_ref[:, A_Q:A_Q + 2 * A_KV])
    for h in range(A_KV_HEADS):
        k_ref[:, h * hd:(h + 1) * hd] = _head_norm(p[:, h * hd:(h + 1) * hd], kg_ref[...])
    v_ref[...] = p[:, A_KV:].astype(BF16)
    u0 = A_Q + 2 * A_KV
    for c in range(B_CH // FF_CHUNK):
        lo = c * FF_CHUNK
        u = _dot(xn, w_ref[:, u0 + lo:u0 + lo + FF_CHUNK])
        gate = _dot(xn, w_ref[:, u0 + B_CH + lo:u0 + B_CH + lo + FF_CHUNK])
        glu_ref[:, lo:lo + FF_CHUNK] = u * jax.nn.sigmoid(gate)


def _even_in(xn, w_in, q_gain, k_gain):
    n, d = xn.shape
    return pl.pallas_call(
        _even_in_kernel,
        out_shape=[jax.ShapeDtypeStruct((n, A_Q), BF16), jax.ShapeDtypeStruct((n, A_KV), BF16),
                   jax.ShapeDtypeStruct((n, A_KV), BF16), jax.ShapeDtypeStruct((n, B_CH), F32)],
        grid=(n // TM,),
        in_specs=[_rows(TM, d), _resident(w_in.shape), _resident((1, A_HEAD_DIM)),
                  _resident((1, A_HEAD_DIM))],
        out_specs=[_rows(TM, A_Q), _rows(TM, A_KV), _rows(TM, A_KV), _rows(TM, B_CH)],
        compiler_params=_params(
            ("parallel",),
            w_in.size * 2 + 2 * TM * (d * 2 + A_Q * 2 + A_KV * 4 + B_CH * 4) + 4 * TM * FF_CHUNK * 4),
        name="even_in",
    )(xn, w_in, q_gain, k_gain)


def _attn_kernel(sink_ref, q_ref, k_ref, v_ref, o_ref, *, seq):
    blk, hd = A_BLOCK, A_HEAD_DIM
    span = 3 * blk
    n = pl.program_id(1)
    start = pl.multiple_of(jnp.clip((n - 1) * blk, 0, seq - span), blk)
    kb = k_ref[0, pl.ds(start, span), :]
    vb = v_ref[0, pl.ds(start, span), :]
    qpos = n * blk + lax.broadcasted_iota(jnp.int32, (blk, span), 0)
    kpos = start + lax.broadcasted_iota(jnp.int32, (blk, span), 1)
    absrel = jnp.abs(kpos - qpos).astype(F32)
    valid = absrel <= float(A_WINDOW)
    scale = hd ** -0.5
    for g in range(A_KV_HEADS):
        kg = kb[:, g * hd:(g + 1) * hd]
        vg = vb[:, g * hd:(g + 1) * hd]
        qg = jnp.concatenate(
            [q_ref[:, (g * A_GROUP + j) * hd:(g * A_GROUP + j + 1) * hd] for j in range(A_GROUP)], axis=0)
        s = _dot_nt(qg, kg) * scale
        probs = []
        for j in range(A_GROUP):
            h = g * A_GROUP + j
            slope = 2.0 ** (-8.0 * (h + 1) / A_HEADS)
            sj = jnp.where(valid, s[j * blk:(j + 1) * blk] - slope * absrel, -jnp.inf)
            sink = sink_ref[h]
            m = jnp.maximum(jnp.max(sj, axis=-1, keepdims=True), sink)
            p = jnp.exp(sj - m)
            den = jnp.sum(p, axis=-1, keepdims=True) + jnp.exp(sink - m)
            probs.append((p * (1.0 / den)).astype(BF16))
        og = _dot(jnp.concatenate(probs, axis=0), vg)
        for j in range(A_GROUP):
            h = g * A_GROUP + j
            o_ref[:, h * hd:(h + 1) * hd] = og[j * blk:(j + 1) * blk].astype(BF16)


def _attention(q, k, v, sink, bsz, seq):
    n = bsz * seq
    nb = seq // A_BLOCK
    k3 = k.reshape(bsz, seq, A_KV)
    v3 = v.reshape(bsz, seq, A_KV)
    return pl.pallas_call(
        functools.partial(_attn_kernel, seq=seq),
        out_shape=jax.ShapeDtypeStruct((n, A_Q), BF16),
        grid=(bsz, nb),
        in_specs=[pl.BlockSpec(memory_space=pltpu.SMEM),
                  pl.BlockSpec((A_BLOCK, A_Q), lambda b, i: (b * nb + i, 0)),
                  pl.BlockSpec((1, seq, A_KV), lambda b, i: (b, 0, 0)),
                  pl.BlockSpec((1, seq, A_KV), lambda b, i: (b, 0, 0))],
        out_specs=pl.BlockSpec((A_BLOCK, A_Q), lambda b, i: (b * nb + i, 0)),
        compiler_params=_params(("parallel", "arbitrary"),
                                4 * seq * A_KV * 2 + 4 * A_BLOCK * A_Q * 2 + 16 * A_BLOCK * 3 * A_BLOCK * 4),
        name="window_attn",
    )(sink, q, k3, v3)


def _halo_specs(width, bsz, seq, tile, halo, reverse):
    nt = seq // tile
    per_tile = tile // halo
    per_seq = seq // halo

    def t_of(i):
        return (nt - 1 - i) if reverse else i

    def prev_map(b, i):
        t = t_of(i)
        return (b * per_seq + jnp.maximum(t * per_tile - 1, 0), 0)

    def main_map(b, i):
        return (b * nt + t_of(i), 0)

    def next_map(b, i):
        t = t_of(i)
        return (b * per_seq + jnp.minimum((t + 1) * per_tile, per_seq - 1), 0)

    return [pl.BlockSpec((halo, width), prev_map), pl.BlockSpec((tile, width), main_map),
            pl.BlockSpec((halo, width), next_map)]


def _fill_padded(xp_ref, prev_ref, main_ref, next_ref, has_prev, has_next):
    halo = prev_ref.shape[0]
    tile = main_ref.shape[0]
    xp_ref[0:halo, :] = jnp.where(has_prev, prev_ref[...], 0.0)
    xp_ref[halo:halo + tile, :] = main_ref[...]
    xp_ref[halo + tile:halo + tile + halo, :] = jnp.where(has_next, next_ref[...], 0.0)


def _convb_kernel(prev_ref, main_ref, next_ref, w_ref, b_ref, g_ref, beta_ref, o_ref, xp_ref, y_ref):
    i = pl.program_id(1)
    _fill_padded(xp_ref, prev_ref, main_ref, next_ref, i > 0, i < pl.num_programs(1) - 1)
    first = CONV_HALO - (B_CONV - 1) // 2
    for r0 in range(0, CONV_T, CONV_RC):
        for c0 in range(0, B_CH, CONV_LC):
            acc = None
            for j in range(B_CONV):
                term = xp_ref[r0 + first + j:r0 + first + j + CONV_RC, c0:c0 + CONV_LC] * w_ref[j:j + 1, c0:c0 + CONV_LC]
                acc = term if acc is None else acc + term
            y_ref[r0:r0 + CONV_RC, c0:c0 + CONV_LC] = acc + b_ref[:, c0:c0 + CONV_LC]
    y = y_ref[...]
    mu = jnp.mean(y, axis=-1, keepdims=True)
    yc = y - mu
    var = jnp.mean(yc * yc, axis=-1, keepdims=True)
    yn = (yc * lax.rsqrt(var + EPS)) * g_ref[...] + beta_ref[...]
    o_ref[...] = (yn * jax.nn.sigmoid(yn)).astype(BF16)


def _conv_b(glu, conv_w, conv_b, norm_g, norm_b, bsz, seq):
    n = bsz * seq
    nt = seq // CONV_T
    vec = _resident((1, B_CH))
    return pl.pallas_call(
        _convb_kernel,
        out_shape=jax.ShapeDtypeStruct((n, B_CH), BF16),
        grid=(bsz, nt),
        in_specs=_halo_specs(B_CH, bsz, seq, CONV_T, CONV_HALO, False) + [_resident((B_CONV, B_CH)), vec, vec, vec],
        out_specs=pl.BlockSpec((CONV_T, B_CH), lambda b, i: (b * nt + i, 0)),
        scratch_shapes=[pltpu.VMEM((CONV_T + 2 * CONV_HALO, B_CH), F32), pltpu.VMEM((CONV_T, B_CH), F32)],
        compiler_params=_params(("parallel", "arbitrary"), 8 * CONV_T * B_CH * 4),
        name="conformer_conv",
    )(glu, glu, glu, conv_w, conv_b, norm_g, norm_b)


def _even_out_kernel(a_ref, c_ref, x_ref, w_ref, g_ref, xo_ref, *maybe_xn):
    xn_ref = maybe_xn[0] if maybe_xn else None
    a = a_ref[...]
    c = c_ref[...]
    _residual_out(x_ref, xo_ref, xn_ref, g_ref, 1.0,
                  lambda lo, hi: _dot(a, w_ref[0:A_Q, lo:hi]) + _dot(c, w_ref[A_Q:A_Q + B_CH, lo:hi]))


def _even_out(a, c, x, w_out, g_next):
    return _out_call(_even_out_kernel, "even_out", [a, c], x, [w_out], g_next, True)


def _odd_in_c_kernel(xn_ref, w_ref, gup_ref, gb_ref, q_ref, k_ref, v_ref, og_ref, laf_ref, lab_ref):
    xn = xn_ref[...]
    q_ref[...] = _dot(xn, w_ref[:, 0:C_QK]) * (C_DK ** -0.5)
    k_ref[...] = _dot(xn, w_ref[:, C_QK:2 * C_QK])
    o = 2 * C_QK
    for c in range(C_V // FF_CHUNK):
        lo = c * FF_CHUNK
        v_ref[:, lo:lo + FF_CHUNK] = _dot(xn, w_ref[:, o + lo:o + lo + FF_CHUNK]).astype(BF16)
        og_ref[:, lo:lo + FF_CHUNK] = _dot(xn, w_ref[:, o + C_V + lo:o + C_V + lo + FF_CHUNK]).astype(BF16)
    o = o + 2 * C_V
    lr = _dot(xn, w_ref[:, o:o + LR_PAD]).astype(BF16)
    pre = _dot(lr, gup_ref[...]) + gb_ref[...]
    loga = _log_sigmoid(pre) * (1.0 / C_TAU)
    laf_ref[...] = loga[:, 0:C_QK]
    lab_ref[...] = loga[:, C_QK:2 * C_QK]


def _odd_in_c(xn, w_c, gup, gbias):
    n, d = xn.shape
    qk = jax.ShapeDtypeStruct((n, C_QK), F32)
    vv = jax.ShapeDtypeStruct((n, C_V), BF16)
    return pl.pallas_call(
        _odd_in_c_kernel,
        out_shape=[qk, qk, vv, vv, qk, qk],
        grid=(n // TM,),
        in_specs=[_rows(TM, d), _resident(w_c.shape), _resident(gup.shape), _resident(gbias.shape)],
        out_specs=[_rows(TM, C_QK), _rows(TM, C_QK), _rows(TM, C_V), _rows(TM, C_V),
                   _rows(TM, C_QK), _rows(TM, C_QK)],
        compiler_params=_params(
            ("parallel",),
            w_c.size * 2 + 2 * TM * (d * 2 + 4 * C_QK * 4 + 2 * C_V * 2) + 6 * TM * FF_CHUNK * 4),
        name="odd_in_c",
    )(xn, w_c, gup, gbias)


def _odd_in_d_kernel(xn_ref, w_ref, xr_ref, yg_ref):
    xn = xn_ref[...]
    for c in range(D_WIDTH // FF_CHUNK):
        lo = c * FF_CHUNK
        xr_ref[:, lo:lo + FF_CHUNK] = _dot(xn, w_ref[:, lo:lo + FF_CHUNK])
        yg_ref[:, lo:lo + FF_CHUNK] = _dot(xn, w_ref[:, D_WIDTH + lo:D_WIDTH + lo + FF_CHUNK]).astype(BF16)


def _odd_in_d(xn, w_d):
    n, d = xn.shape
    return pl.pallas_call(
        _odd_in_d_kernel,
        out_shape=[jax.ShapeDtypeStruct((n, D_WIDTH), F32), jax.ShapeDtypeStruct((n, D_WIDTH), BF16)],
        grid=(n // TM,),
        in_specs=[_rows(TM, d), _resident(w_d.shape)],
        out_specs=[_rows(TM, D_WIDTH), _rows(TM, D_WIDTH)],
        compiler_params=_params(("parallel",), w_d.size * 2 + 2 * TM * (d * 2 + D_WIDTH * 6) + 4 * TM * FF_CHUNK * 4),
        name="odd_in_d",
    )(xn, w_d)


def _split3(x):
    hi = x.astype(BF16)
    r1 = x - hi.astype(F32)
    mid = r1.astype(BF16)
    lo = (r1 - mid.astype(F32)).astype(BF16)
    return hi, mid, lo


def _gla_chunk(q_all, k_all, v_all, la_all, st_ref, reverse):
    cc, sub = GLA_C, GLA_SUB
    nsub = cc // sub
    row = lax.broadcasted_iota(jnp.int32, (cc, cc), 0)
    col = lax.broadcasted_iota(jnp.int32, (cc, cc), 1)
    keep_mask = (col >= row) if reverse else (col <= row)
    tri = keep_mask.astype(BF16)
    hi, mid, lo = _split3(la_all)
    b_all = _dot(tri, hi) + _dot(tri, mid) + _dot(tri, lo)
    r128 = lax.broadcasted_iota(jnp.int32, (cc, C_DK), 0)
    rblk = lax.broadcasted_iota(jnp.int32, (cc, V7X_LANES), 0) // sub
    zero_row = jnp.zeros((1, C_DK), F32)
    outs = []
    for h in range(C_HEADS):
        b = b_all[:, h * C_DK:(h + 1) * C_DK]
        q = q_all[:, h * C_DK:(h + 1) * C_DK]
        k = k_all[:, h * C_DK:(h + 1) * C_DK]
        v = v_all[:, h * C_DV:(h + 1) * C_DV]
        if reverse:
            refs = [b[(i + 1) * sub:(i + 1) * sub + 1] for i in range(nsub - 1)] + [zero_row]
            b_end = b[0:1]
        else:
            refs = [zero_row] + [b[i * sub - 1:i * sub] for i in range(1, nsub)]
            b_end = b[cc - 1:cc]
        ref_rows = jnp.concatenate([jnp.broadcast_to(r, (sub, C_DK)) for r in refs], axis=0)
        qt = q * jnp.exp(b - ref_rows)
        q_inter = qt * jnp.exp(ref_rows)
        kvars = []
        for i in range(nsub):
            seen = (r128 >= i * sub) if reverse else (r128 < (i + 1) * sub)
            kv = k * jnp.exp(jnp.where(seen, refs[i] - b, -1e30))
            kvars.append(kv.astype(BF16))
            kvars.append(jnp.zeros((V7X_LANES - cc, C_DK), BF16))
        kcat = jnp.concatenate(kvars, axis=0)
        r_all = _dot_nt(qt.astype(BF16), kcat)
        att = r_all[:, (nsub - 1) * V7X_LANES:nsub * V7X_LANES]
        for i in range(nsub - 2, -1, -1):
            att = jnp.where(rblk == i, r_all[:, i * V7X_LANES:(i + 1) * V7X_LANES], att)
        att = jnp.where(keep_mask, att[:, 0:cc], 0.0).astype(BF16)
        st = st_ref[h]
        o = _dot(att, v) + _dot_nt(q_inter.astype(BF16), st.astype(BF16))
        k_out = (k * jnp.exp(b_end - b)).astype(BF16)
        st_ref[h] = jnp.exp(b_end) * st + _dot_tn(v, k_out)
        outs.append(o)
    return jnp.concatenate(outs, axis=1)


def _gla_kernel(qf_ref, kf_ref, vf_ref, laf_ref, qb_ref, kb_ref, vb_ref, lab_ref, of_ref, ob_ref, sf_ref, sb_ref):
    @pl.when(pl.program_id(1) == 0)
    def _():
        sf_ref[...] = jnp.zeros_like(sf_ref)
        sb_ref[...] = jnp.zeros_like(sb_ref)

    nch = GLA_T // GLA_C
    for c in range(nch):
        rf = slice(c * GLA_C, (c + 1) * GLA_C)
        of_ref[rf, :] = _gla_chunk(qf_ref[rf, :], kf_ref[rf, :], vf_ref[rf, :], laf_ref[rf, :],
                                   sf_ref, False).astype(BF16)
        rb = slice((nch - 1 - c) * GLA_C, (nch - c) * GLA_C)
        ob_ref[rb, :] = _gla_chunk(qb_ref[rb, :], kb_ref[rb, :], vb_ref[rb, :], lab_ref[rb, :],
                                   sb_ref, True).astype(BF16)


def _gla(q, k, v, la_f, la_b, bsz, seq):
    n = bsz * seq
    nt = seq // GLA_T

    def fwd(width):
        return pl.BlockSpec((GLA_T, width), lambda b, i: (b * nt + i, 0))

    def bwd(width):
        return pl.BlockSpec((GLA_T, width), lambda b, i: (b * nt + nt - 1 - i, 0))

    o = jax.ShapeDtypeStruct((n, C_V), BF16)
    st = pltpu.VMEM((C_HEADS, C_DV, C_DK), F32)
    return pl.pallas_call(
        _gla_kernel,
        out_shape=[o, o],
        grid=(bsz, nt),
        in_specs=[fwd(C_QK), fwd(C_QK), fwd(C_V), fwd(C_QK), bwd(C_QK), bwd(C_QK), bwd(C_V), bwd(C_QK)],
        out_specs=[fwd(C_V), bwd(C_V)],
        scratch_shapes=[st, st],
        compiler_params=_params(("parallel", "arbitrary"), 4 * GLA_T * (3 * C_QK * 4 + 2 * C_V * 2) + (8 << 20)),
        name="gla",
    )(q, k, v, la_f, q, k, v, la_b)


def _lin_scan(a, u, carry, reverse):
    t = a.shape[0]
    r8 = lax.broadcasted_iota(jnp.int32, a.shape, 0) & (V7X_SUBLANES - 1)
    d = 1
    while d < V7X_SUBLANES:
        if reverse:
            ush, ash, ok = pltpu.roll(u, t - d, 0), pltpu.roll(a, t - d, 0), r8 < V7X_SUBLANES - d
        else:
            ush, ash, ok = pltpu.roll(u, d, 0), pltpu.roll(a, d, 0), r8 >= d
        u = u + a * jnp.where(ok, ush, 0.0)
        a = a * jnp.where(ok, ash, 1.0)
        d *= 2
    ngroups = t // V7X_SUBLANES
    hs = [None] * ngroups
    for g in (range(ngroups - 1, -1, -1) if reverse else range(ngroups)):
        rows = slice(g * V7X_SUBLANES, (g + 1) * V7X_SUBLANES)
        hg = u[rows] + a[rows] * carry
        hs[g] = hg
        carry = hg[0:1] if reverse else hg[V7X_SUBLANES - 1:V7X_SUBLANES]
    return jnp.concatenate(hs, axis=0), carry


def _rglru_dir(xp_ref, cw_ref, cb_ref, w_ref, ba_ref, bx_ref, lam_ref, o_ref, carry_ref, reverse):
    first = V7X_SUBLANES - (D_CONV - 1) // 2
    for blk in range(D_BLOCKS):
        cols = slice(blk * D_BLOCK_DIM, (blk + 1) * D_BLOCK_DIM)
        xc = cb_ref[:, cols]
        for j in range(D_CONV):
            xc = xc + xp_ref[first + j:first + j + CONV_T, cols] * cw_ref[j:j + 1, cols]
        pr = _dot(xc.astype(BF16), w_ref[blk])
        r = jax.nn.sigmoid(pr[:, 0:D_BLOCK_DIM] + ba_ref[:, cols])
        gate_i = jax.nn.sigmoid(pr[:, D_BLOCK_DIM:] + bx_ref[:, cols])
        log_a = (D_C * r) * _log_sigmoid(lam_ref[:, cols])
        a = jnp.exp(log_a)
        th = jnp.tanh(log_a)
        u = jnp.sqrt(-2.0 * th / (1.0 - th)) * (gate_i * xc)
        h, carry = _lin_scan(a, u, carry_ref[:, cols], reverse)
        carry_ref[:, cols] = carry
        o_ref[:, cols] = h.astype(BF16)


def _rglru_kernel(pf_ref, mf_ref, nf_ref, pb_ref, mb_ref, nb_ref, cw_ref, cb_ref, wf_ref, wb_ref,
                  ba_ref, bx_ref, lam_ref, hf_ref, hb_ref, xp_ref, cf_ref, cr_ref):
    i = pl.program_id(1)
    last = pl.num_programs(1) - 1

    @pl.when(i == 0)
    def _():
        cf_ref[...] = jnp.zeros_like(cf_ref)
        cr_ref[...] = jnp.zeros_like(cr_ref)

    _fill_padded(xp_ref, pf_ref, mf_ref, nf_ref, i > 0, i < last)
    _rglru_dir(xp_ref, cw_ref, cb_ref, wf_ref, ba_ref.at[0], bx_ref.at[0], lam_ref.at[0], hf_ref, cf_ref, False)
    _fill_padded(xp_ref, pb_ref, mb_ref, nb_ref, i < last, i > 0)
    _rglru_dir(xp_ref, cw_ref, cb_ref, wb_ref, ba_ref.at[1], bx_ref.at[1], lam_ref.at[1], hb_ref, cr_ref, True)


def _rglru(xr, conv_w, conv_b, w_f, w_b, ba, bx, lam, bsz, seq):
    n = bsz * seq
    nt = seq // CONV_T
    o = jax.ShapeDtypeStruct((n, D_WIDTH), BF16)
    vec2 = _resident((2, 1, D_WIDTH))
    carry = pltpu.VMEM((1, D_WIDTH), F32)
    return pl.pallas_call(
        _rglru_kernel,
        out_shape=[o, o],
        grid=(bsz, nt),
        in_specs=(_halo_specs(D_WIDTH, bsz, seq, CONV_T, V7X_SUBLANES, False)
                  + _halo_specs(D_WIDTH, bsz, seq, CONV_T, V7X_SUBLANES, True)
                  + [_resident((D_CONV, D_WIDTH)), _resident((1, D_WIDTH)), _resident(w_f.shape),
                     _resident(w_b.shape), vec2, vec2, vec2]),
        out_specs=[pl.BlockSpec((CONV_T, D_WIDTH), lambda b, i: (b * nt + i, 0)),
                   pl.BlockSpec((CONV_T, D_WIDTH), lambda b, i: (b * nt + nt - 1 - i, 0))],
        scratch_shapes=[pltpu.VMEM((CONV_T + 2 * V7X_SUBLANES, D_WIDTH), F32), carry, carry],
        compiler_params=_params(("parallel", "arbitrary"), 12 * CONV_T * D_WIDTH * 4 + (8 << 20)),
        name="rglru",
    )(xr, xr, xr, xr, xr, xr, conv_w, conv_b, w_f, w_b, ba, bx, lam)


def _odd_out_kernel(of_ref, ob_ref, og_ref, hf_ref, hb_ref, yg_ref, x_ref, w_ref, cg_ref, g_ref, xo_ref, *maybe_xn):
    xn_ref = maybe_xn[0] if maybe_xn else None
    c_parts = []
    for h in range(C_HEADS):
        cols = slice(h * C_DV, (h + 1) * C_DV)
        o = of_ref[:, cols].astype(F32) + ob_ref[:, cols].astype(F32)
        og = og_ref[:, cols].astype(F32)
        c_parts.append((_rms(o, cg_ref[...]) * (og * jax.nn.sigmoid(og))).astype(BF16))
    c_out = jnp.concatenate(c_parts, axis=1)
    hsum = hf_ref[...].astype(F32) + hb_ref[...].astype(F32)
    d_out = (hsum * jax.nn.gelu(yg_ref[...].astype(F32))).astype(BF16)
    _residual_out(x_ref, xo_ref, xn_ref, g_ref, 1.0,
                  lambda lo, hi: _dot(c_out, w_ref[0:C_V, lo:hi]) + _dot(d_out, w_ref[C_V:C_V + D_WIDTH, lo:hi]))


def _odd_out(o_f, o_b, og, h_f, h_b, yg, x, w_out, c_norm_g, g_next):
    return _out_call(_odd_out_kernel, "odd_out", [o_f, o_b, og, h_f, h_b, yg], x, [w_out, c_norm_g], g_next, True)


def _prepare(p):
    row = lambda v: v.reshape(1, -1).astype(F32)
    layers = []
    for layer in range(DEPTH):
        j = layer // 2
        lw = {
            "ln_ffn1": row(p["ln_ffn1"][layer]), "ln_mix": row(p["ln_mix"][layer]), "ln_ffn2": row(p["ln_ffn2"][layer]),
            "ffn1_w_in": p["ffn1_w_in"][layer].astype(BF16), "ffn1_w_out": p["ffn1_w_out"][layer].astype(BF16),
            "ffn2_w_in": p["ffn2_w_in"][layer].astype(BF16), "ffn2_w_out": p["ffn2_w_out"][layer].astype(BF16),
        }
        if layer % 2 == 0:
            lw.update({
                "w_in": p["ev_w_in"][j].astype(BF16), "w_out": p["ev_w_out"][j].astype(BF16),
                "q_gain": row(p["a_q_gain"][j]), "k_gain": row(p["a_k_gain"][j]), "sink": p["a_sink"][j].astype(F32),
                "conv_w": p["b_conv_w"][j].astype(F32), "conv_b": row(p["b_conv_b"][j]),
                "norm_g": row(p["b_norm_g"][j]), "norm_b": row(p["b_norm_b"][j]),
            })
        else:
            w = p["od_w_in"][j]
            o4 = 2 * C_QK + 2 * C_V
            o5 = o4 + 2 * C_RANK
            w_c = jnp.concatenate([w[:, :o4], w[:, o4:o5], jnp.zeros((D_MODEL, LR_PAD - 2 * C_RANK), w.dtype)], axis=1)
            gup = jnp.zeros((LR_PAD, 2 * C_QK), F32)
            gup = gup.at[0:C_RANK, 0:C_QK].set(p["c_gate_up"][j, 0])
            gup = gup.at[C_RANK:2 * C_RANK, C_QK:2 * C_QK].set(p["c_gate_up"][j, 1])
            lw.update({
                "w_c": w_c.astype(BF16), "w_d": w[:, o5:].astype(BF16), "w_out": p["od_w_out"][j].astype(BF16),
                "gup": gup.astype(BF16), "gbias": p["c_gate_bias"][j].reshape(1, 2 * C_QK).astype(F32),
                "c_norm_g": row(p["c_norm_g"][j]),
                "conv_w": p["d_conv_w"][j].astype(F32), "conv_b": row(p["d_conv_b"][j]),
                "w_f": jnp.concatenate([p["d_wa"][j, 0], p["d_wx"][j, 0]], axis=-1).astype(BF16),
                "w_b": jnp.concatenate([p["d_wa"][j, 1], p["d_wx"][j, 1]], axis=-1).astype(BF16),
                "ba": p["d_ba"][j].reshape(2, 1, D_WIDTH).astype(F32), "bx": p["d_bx"][j].reshape(2, 1, D_WIDTH).astype(F32),
                "lam": p["d_lambda"][j].reshape(2, 1, D_WIDTH).astype(F32),
            })
        layers.append(lw)
    return layers


def _run_trunk(x3, layers):
    bsz, seq, d = x3.shape
    x = x3.reshape(bsz * seq, d)
    xn = _norm(x, layers[0]["ln_ffn1"])
    for layer, lw in enumerate(layers):
        act = _ffn_in(xn, lw["ffn1_w_in"])
        x, xn = _ffn_out(act, x, lw["ffn1_w_out"], lw["ln_mix"], True)
        if layer % 2 == 0:
            q, k, v, glu = _even_in(xn, lw["w_in"], lw["q_gain"], lw["k_gain"])
            a = _attention(q, k, v, lw["sink"], bsz, seq)
            c = _conv_b(glu, lw["conv_w"], lw["conv_b"], lw["norm_g"], lw["norm_b"], bsz, seq)
            x, xn = _even_out(a, c, x, lw["w_out"], lw["ln_ffn2"])
        else:
            q, k, v, og, la_f, la_b = _odd_in_c(xn, lw["w_c"], lw["gup"], lw["gbias"])
            xr, yg = _odd_in_d(xn, lw["w_d"])
            o_f, o_b = _gla(q, k, v, la_f, la_b, bsz, seq)
            h_f, h_b = _rglru(xr, lw["conv_w"], lw["conv_b"], lw["w_f"], lw["w_b"], lw["ba"], lw["bx"], lw["lam"],
                              bsz, seq)
            x, xn = _odd_out(o_f, o_b, og, h_f, h_b, yg, x, lw["w_out"], lw["c_norm_g"], lw["ln_ffn2"])
        act = _ffn_in(xn, lw["ffn2_w_in"])
        last = layer == len(layers) - 1
        g_next = lw["ln_ffn2"] if last else layers[layer + 1]["ln_ffn1"]
        x, xn = _ffn_out(act, x, lw["ffn2_w_out"], g_next, not last)
    return x.reshape(bsz, seq, d)


def kernel(x_prompt, x_sample, ln_ffn1, ffn1_w_in, ffn1_w_out, ln_mix, ln_ffn2, ffn2_w_in, ffn2_w_out, ev_w_in, ev_w_out, a_q_gain, a_k_gain, a_sink, b_conv_w, b_conv_b, b_norm_g, b_norm_b, od_w_in, od_w_out, c_gate_up, c_gate_bias, c_norm_g, d_conv_w, d_conv_b, d_wa, d_ba, d_wx, d_bx, d_lambda):
    layers = _prepare(dict(
        ln_ffn1=ln_ffn1, ffn1_w_in=ffn1_w_in, ffn1_w_out=ffn1_w_out, ln_mix=ln_mix, ln_ffn2=ln_ffn2,
        ffn2_w_in=ffn2_w_in, ffn2_w_out=ffn2_w_out, ev_w_in=ev_w_in, ev_w_out=ev_w_out, a_q_gain=a_q_gain,
        a_k_gain=a_k_gain, a_sink=a_sink, b_conv_w=b_conv_w, b_conv_b=b_conv_b, b_norm_g=b_norm_g,
        b_norm_b=b_norm_b, od_w_in=od_w_in, od_w_out=od_w_out, c_gate_up=c_gate_up, c_gate_bias=c_gate_bias,
        c_norm_g=c_norm_g, d_conv_w=d_conv_w, d_conv_b=d_conv_b, d_wa=d_wa, d_ba=d_ba, d_wx=d_wx, d_bx=d_bx,
        d_lambda=d_lambda))
    return (_run_trunk(x_prompt, layers), _run_trunk(x_sample, layers))
```

```python
import functools

import jax
import jax.numpy as jnp
from jax import lax
from jax.experimental import pallas as pl
from jax.experimental.pallas import tpu as pltpu

F32 = jnp.float32
BF16 = jnp.bfloat16

D_MODEL = 4096
DEPTH = 4
D_FF = 3 * D_MODEL // 8
EPS = 1e-6
MIX_HALF = D_MODEL // 4
A_HEAD_DIM = 128
A_HEADS = MIX_HALF // A_HEAD_DIM
A_KV_HEADS = 2
A_GROUP = A_HEADS // A_KV_HEADS
A_WINDOW = 128
A_BLOCK = 128
A_Q = A_HEADS * A_HEAD_DIM
A_KV = A_KV_HEADS * A_HEAD_DIM
B_CH = MIX_HALF
B_CONV = 31
C_HEADS = 4
C_DV = MIX_HALF // C_HEADS
C_DK = C_DV // 2
C_RANK = 16
C_TAU = 16.0
C_QK = C_HEADS * C_DK
C_V = C_HEADS * C_DV
D_WIDTH = MIX_HALF
D_BLOCKS = 8
D_BLOCK_DIM = D_WIDTH // D_BLOCKS
D_CONV = 4
D_C = 8.0

V7X_LANES = 128
V7X_SUBLANES = 8
V7X_VMEM_BYTES = 64 * 1024 * 1024

TM = 256
TM_NORM = 512
FF_CHUNK = 512
OUT_CHUNK = 1024
CONV_T = 256
CONV_HALO = 16
CONV_RC = 128
CONV_STRIDE = 2
RG_SEG = 4
GLA_T = 256
GLA_C = 64
GLA_SUB = 16
LR_PAD = 128


def _vmem_limit(nbytes):
    return int(min(nbytes * 5 // 4 + (4 << 20), V7X_VMEM_BYTES - (4 << 20)))


def _params(sem, nbytes):
    return pltpu.CompilerParams(dimension_semantics=sem, vmem_limit_bytes=_vmem_limit(nbytes))


def _resident(shape):
    nd = len(shape)
    return pl.BlockSpec(shape, lambda *_: (0,) * nd, pipeline_mode=pl.Buffered(1))


def _layer_spec(w):
    stack, layer = w
    nd = stack.ndim
    return pl.BlockSpec((None,) + stack.shape[1:], lambda *_: (layer,) + (0,) * (nd - 1),
                        pipeline_mode=pl.Buffered(1))


def _layer_bytes(w):
    stack = w[0]
    return stack.size // stack.shape[0] * stack.dtype.itemsize


def _rows(tm, width):
    return pl.BlockSpec((tm, width), lambda i: (i, 0))


def _dot(a, b):
    return jnp.dot(a, b, preferred_element_type=F32)


def _dot_nt(a, b):
    return lax.dot_general(a, b, (((1,), (1,)), ((), ())), preferred_element_type=F32)


def _dot_tn(a, b):
    return lax.dot_general(a, b, (((0,), (0,)), ((), ())), preferred_element_type=F32)


def _rms(x, g):
    inv = lax.rsqrt(jnp.mean(x * x, axis=-1, keepdims=True) + EPS)
    return (x * inv) * g


def _log_sigmoid(x):
    return jnp.minimum(x, 0.0) - jnp.log1p(jnp.exp(-jnp.abs(x)))


def _norm_kernel(x_ref, g_ref, xg_ref, inv_ref):
    x = x_ref[...]
    xg_ref[...] = (x * g_ref[...]).astype(BF16)
    inv = lax.rsqrt(jnp.mean(x * x, axis=-1, keepdims=True) + EPS)
    inv_ref[...] = jnp.broadcast_to(inv, inv_ref.shape)


def _norm(x, g):
    n, d = x.shape
    return tuple(pl.pallas_call(
        _norm_kernel,
        out_shape=[jax.ShapeDtypeStruct((n, d), BF16), jax.ShapeDtypeStruct((n, V7X_LANES), F32)],
        grid=(n // TM_NORM,),
        in_specs=[_rows(TM_NORM, d), _resident((1, d))],
        out_specs=[_rows(TM_NORM, d), _rows(TM_NORM, V7X_LANES)],
        compiler_params=_params(("parallel",), 2 * TM_NORM * d * 6),
        name="rmsnorm",
    )(x, g))


def _scaled_dot(xg, w, inv):
    return _dot(xg, w) * jnp.concatenate([inv] * (w.shape[1] // V7X_LANES), axis=1)


def _norm_specs(d):
    return [_rows(TM, d), _rows(TM, V7X_LANES)]


def _ffn_in_kernel(xg_ref, inv_ref, w_ref, act_ref):
    xg = xg_ref[...]
    inv = inv_ref[...]
    for c in range(D_FF // FF_CHUNK):
        lo = c * FF_CHUNK
        gate = _scaled_dot(xg, w_ref[:, lo:lo + FF_CHUNK], inv)
        up = _scaled_dot(xg, w_ref[:, D_FF + lo:D_FF + lo + FF_CHUNK], inv)
        act_ref[:, lo:lo + FF_CHUNK] = (gate * jax.nn.sigmoid(gate) * up).astype(BF16)


def _ffn_in(xn, w_in):
    n, d = xn[0].shape
    return pl.pallas_call(
        _ffn_in_kernel,
        out_shape=jax.ShapeDtypeStruct((n, D_FF), BF16),
        grid=(n // TM,),
        in_specs=_norm_specs(d) + [_layer_spec(w_in)],
        out_specs=_rows(TM, D_FF),
        compiler_params=_params(("parallel",), _layer_bytes(w_in) + 2 * TM * (d + D_FF) * 2 + 4 * TM * FF_CHUNK * 4),
        name="ffn_in",
    )(*xn, w_in[0])


def _residual_out(x_ref, xo_ref, norm_refs, g_ref, scale, branch_fn):
    tm = x_ref.shape[0]
    ssq = jnp.zeros((tm, 1), F32)
    for c in range(D_MODEL // OUT_CHUNK):
        lo = c * OUT_CHUNK
        y = branch_fn(lo, lo + OUT_CHUNK)
        if scale != 1.0:
            y = scale * y
        xnew = x_ref[:, lo:lo + OUT_CHUNK] + y
        xo_ref[:, lo:lo + OUT_CHUNK] = xnew
        if norm_refs:
            norm_refs[0][:, lo:lo + OUT_CHUNK] = (xnew * g_ref[:, lo:lo + OUT_CHUNK]).astype(BF16)
            ssq = ssq + jnp.sum(xnew * xnew, axis=-1, keepdims=True)
    if norm_refs:
        inv = lax.rsqrt(ssq * (1.0 / D_MODEL) + EPS)
        norm_refs[1][...] = jnp.broadcast_to(inv, norm_refs[1].shape)


def _ffn_out_kernel(act_ref, x_ref, w_ref, g_ref, xo_ref, *norm_refs):
    act = act_ref[...]
    _residual_out(x_ref, xo_ref, norm_refs, g_ref, 0.5, lambda lo, hi: _dot(act, w_ref[:, lo:hi]))


def _out_call(kernel, name, row_inputs, x, w_out, vectors, g_next, emit_norm):
    n, d = x.shape
    out_shape = [jax.ShapeDtypeStruct((n, d), F32)]
    out_specs = [_rows(TM, d)]
    if emit_norm:
        out_shape += [jax.ShapeDtypeStruct((n, d), BF16), jax.ShapeDtypeStruct((n, V7X_LANES), F32)]
        out_specs += _norm_specs(d)
    nbytes = _layer_bytes(w_out)
    nbytes += 2 * sum(TM * a.shape[1] * a.dtype.itemsize for a in row_inputs)
    nbytes += 2 * TM * d * (4 + 4 + 2) + 2 * TM * OUT_CHUNK * 4
    res = pl.pallas_call(
        kernel,
        out_shape=out_shape,
        grid=(n // TM,),
        in_specs=([_rows(TM, a.shape[1]) for a in row_inputs] + [_rows(TM, d)]
                  + [_layer_spec(w_out)] + [_resident(v.shape) for v in vectors] + [_resident((1, d))]),
        out_specs=out_specs,
        compiler_params=_params(("parallel",), nbytes),
        name=name,
    )(*row_inputs, x, w_out[0], *vectors, g_next)
    return (res[0], (res[1], res[2])) if emit_norm else (res[0], None)


def _ffn_out(act, x, w_out, g_next, emit_norm):
    return _out_call(_ffn_out_kernel, "ffn_out", [act], x, w_out, [], g_next, emit_norm)


def _head_norm(p, gain):
    inv = lax.rsqrt(jnp.mean(p * p, axis=-1, keepdims=True) + EPS)
    return ((p * inv) * gain).astype(BF16)


def _even_in_kernel(xg_ref, inv_ref, w_ref, qg_ref, kg_ref, q_ref, k_ref, v_ref, glu_ref):
    xg = xg_ref[...]
    inv = inv_ref[...]
    hd = A_HEAD_DIM
    for c in range(A_Q // FF_CHUNK):
        p = _scaled_dot(xg, w_ref[:, c * FF_CHUNK:(c + 1) * FF_CHUNK], inv)
        for h in range(FF_CHUNK // hd):
            q_ref[:, c * FF_CHUNK + h * hd:c * FF_CHUNK + (h + 1) * hd] = _head_norm(
                p[:, h * hd:(h + 1) * hd], qg_ref[...])
    p = _scaled_dot(xg, w_ref[:, A_Q:A_Q + 2 * A_KV], inv)
    for h in range(A_KV_HEADS):
        k_ref[:, h * hd:(h + 1) * hd] = _head_norm(p[:, h * hd:(h + 1) * hd], kg_ref[...])
    v_ref[...] = p[:, A_KV:].astype(BF16)
    u0 = A_Q + 2 * A_KV
    for c in range(B_CH // FF_CHUNK):
        lo = c * FF_CHUNK
        u = _scaled_dot(xg, w_ref[:, u0 + lo:u0 + lo + FF_CHUNK], inv)
        gate = _scaled_dot(xg, w_ref[:, u0 + B_CH + lo:u0 + B_CH + lo + FF_CHUNK], inv)
        glu_ref[:, lo:lo + FF_CHUNK] = u * jax.nn.sigmoid(gate)


def _even_in(xn, w_in, q_gain, k_gain):
    n, d = xn[0].shape
    return pl.pallas_call(
        _even_in_kernel,
        out_shape=[jax.ShapeDtypeStruct((n, A_Q), BF16), jax.ShapeDtypeStruct((n, A_KV), BF16),
                   jax.ShapeDtypeStruct((n, A_KV), BF16), jax.ShapeDtypeStruct((n, B_CH), F32)],
        grid=(n // TM,),
        in_specs=_norm_specs(d) + [_layer_spec(w_in), _resident((1, A_HEAD_DIM)),
                                   _resident((1, A_HEAD_DIM))],
        out_specs=[_rows(TM, A_Q), _rows(TM, A_KV), _rows(TM, A_KV), _rows(TM, B_CH)],
        compiler_params=_params(
            ("parallel",),
            _layer_bytes(w_in) + 2 * TM * (d * 2 + A_Q * 2 + A_KV * 4 + B_CH * 4) + 4 * TM * FF_CHUNK * 4),
        name="even_in",
    )(*xn, w_in[0], q_gain, k_gain)


def _attn_kernel(sink_ref, q_ref, k_ref, v_ref, o_ref, *, seq):
    blk, hd = A_BLOCK, A_HEAD_DIM
    span = 3 * blk
    n = pl.program_id(1)
    start = pl.multiple_of(jnp.clip((n - 1) * blk, 0, seq - span), blk)
    kb = k_ref[0, pl.ds(start, span), :]
    vb = v_ref[0, pl.ds(start, span), :]
    qpos = n * blk + lax.broadcasted_iota(jnp.int32, (blk, span), 0)
    kpos = start + lax.broadcasted_iota(jnp.int32, (blk, span), 1)
    absrel = jnp.abs(kpos - qpos).astype(F32)
    valid = absrel <= float(A_WINDOW)
    scale = hd ** -0.5
    for g in range(A_KV_HEADS):
        kg = kb[:, g * hd:(g + 1) * hd]
        vg = vb[:, g * hd:(g + 1) * hd]
        qg = jnp.concatenate(
            [q_ref[:, (g * A_GROUP + j) * hd:(g * A_GROUP + j + 1) * hd] for j in range(A_GROUP)], axis=0)
        s = _dot_nt(qg, kg) * scale
        probs = []
        for j in range(A_GROUP):
            h = g * A_GROUP + j
            slope = 2.0 ** (-8.0 * (h + 1) / A_HEADS)
            sj = jnp.where(valid, s[j * blk:(j + 1) * blk] - slope * absrel, -jnp.inf)
            sink = sink_ref[h]
            m = jnp.maximum(jnp.max(sj, axis=-1, keepdims=True), sink)
            p = jnp.exp(sj - m)
            den = jnp.sum(p, axis=-1, keepdims=True) + jnp.exp(sink - m)
            probs.append((p * (1.0 / den)).astype(BF16))
        og = _dot(jnp.concatenate(probs, axis=0), vg)
        for j in range(A_GROUP):
            h = g * A_GROUP + j
            o_ref[:, h * hd:(h + 1) * hd] = og[j * blk:(j + 1) * blk].astype(BF16)


def _attention(q, k, v, sink, bsz, seq):
    n = bsz * seq
    nb = seq // A_BLOCK
    k3 = k.reshape(bsz, seq, A_KV)
    v3 = v.reshape(bsz, seq, A_KV)
    return pl.pallas_call(
        functools.partial(_attn_kernel, seq=seq),
        out_shape=jax.ShapeDtypeStruct((n, A_Q), BF16),
        grid=(bsz, nb),
        in_specs=[pl.BlockSpec(memory_space=pltpu.SMEM),
                  pl.BlockSpec((A_BLOCK, A_Q), lambda b, i: (b * nb + i, 0)),
                  pl.BlockSpec((1, seq, A_KV), lambda b, i: (b, 0, 0)),
                  pl.BlockSpec((1, seq, A_KV), lambda b, i: (b, 0, 0))],
        out_specs=pl.BlockSpec((A_BLOCK, A_Q), lambda b, i: (b * nb + i, 0)),
        compiler_params=_params(("parallel", "arbitrary"),
                                4 * seq * A_KV * 2 + 4 * A_BLOCK * A_Q * 2 + 16 * A_BLOCK * 3 * A_BLOCK * 4),
        name="window_attn",
    )(sink, q, k3, v3)


def _halo_specs(width, bsz, seq, tile, halo, reverse):
    nt = seq // tile
    per_tile = tile // halo
    per_seq = seq // halo

    def t_of(i):
        return (nt - 1 - i) if reverse else i

    def prev_map(b, i):
        t = t_of(i)
        return (b * per_seq + jnp.maximum(t * per_tile - 1, 0), 0)

    def main_map(b, i):
        return (b * nt + t_of(i), 0)

    def next_map(b, i):
        t = t_of(i)
        return (b * per_seq + jnp.minimum((t + 1) * per_tile, per_seq - 1), 0)

    return [pl.BlockSpec((halo, width), prev_map), pl.BlockSpec((tile, width), main_map),
            pl.BlockSpec((halo, width), next_map)]


def _fill_padded_blocks(xp_ref, prev_ref, main_ref, next_ref, has_prev, has_next):
    halo = prev_ref.shape[0]
    tile = main_ref.shape[0]
    for s in range(xp_ref.shape[0]):
        cols = slice(s * V7X_LANES, (s + 1) * V7X_LANES)
        xp_ref[s, 0:halo, :] = jnp.where(has_prev, prev_ref[:, cols], 0.0)
        xp_ref[s, halo:halo + tile, :] = main_ref[:, cols]
        xp_ref[s, halo + tile:halo + tile + halo, :] = jnp.where(has_next, next_ref[:, cols], 0.0)


def _convb_kernel(prev_ref, main_ref, next_ref, w_ref, b_ref, g_ref, beta_ref, o_ref, xp_ref, y_ref):
    i = pl.program_id(1)
    _fill_padded_blocks(xp_ref, prev_ref, main_ref, next_ref, i > 0, i < pl.num_programs(1) - 1)
    first = CONV_HALO - (B_CONV - 1) // 2
    nrows = CONV_RC // CONV_STRIDE
    groups = nrows // V7X_SUBLANES
    phases = [(r0, p) for r0 in range(0, CONV_T, CONV_RC) for p in range(CONV_STRIDE)]
    for s in range(B_CH // V7X_LANES):
        cols = slice(s * V7X_LANES, (s + 1) * V7X_LANES)
        accs = [None] * len(phases)
        for j in range(B_CONV):
            wj = jnp.broadcast_to(w_ref[j:j + 1, cols], (V7X_SUBLANES, V7X_LANES))[None]
            for idx, (r0, p) in enumerate(phases):
                x = xp_ref[s, pl.ds(r0 + p + first + j, nrows, stride=CONV_STRIDE), :]
                term = x.reshape(groups, V7X_SUBLANES, V7X_LANES) * wj
                accs[idx] = term if accs[idx] is None else accs[idx] + term
        for idx, (r0, p) in enumerate(phases):
            y_ref[s, pl.ds(r0 + p, nrows, stride=CONV_STRIDE), :] = (
                accs[idx].reshape(nrows, V7X_LANES) + b_ref[:, cols])
    y = jnp.concatenate([y_ref[s] for s in range(B_CH // V7X_LANES)], axis=1)
    mu = jnp.mean(y, axis=-1, keepdims=True)
    yc = y - mu
    var = jnp.mean(yc * yc, axis=-1, keepdims=True)
    yn = (yc * lax.rsqrt(var + EPS)) * g_ref[...] + beta_ref[...]
    o_ref[...] = (yn * jax.nn.sigmoid(yn)).astype(BF16)


def _conv_b(glu, conv_w, conv_b, norm_g, norm_b, bsz, seq):
    n = bsz * seq
    nt = seq // CONV_T
    vec = _resident((1, B_CH))
    return pl.pallas_call(
        _convb_kernel,
        out_shape=jax.ShapeDtypeStruct((n, B_CH), BF16),
        grid=(bsz, nt),
        in_specs=_halo_specs(B_CH, bsz, seq, CONV_T, CONV_HALO, False) + [_resident((B_CONV, B_CH)), vec, vec, vec],
        out_specs=pl.BlockSpec((CONV_T, B_CH), lambda b, i: (b * nt + i, 0)),
        scratch_shapes=[pltpu.VMEM((B_CH // V7X_LANES, CONV_T + 2 * CONV_HALO, V7X_LANES), F32),
                        pltpu.VMEM((B_CH // V7X_LANES, CONV_T, V7X_LANES), F32)],
        compiler_params=_params(("parallel", "arbitrary"), 8 * CONV_T * B_CH * 4),
        name="conformer_conv",
    )(glu, glu, glu, conv_w, conv_b, norm_g, norm_b)


def _even_out_kernel(a_ref, c_ref, x_ref, w_ref, g_ref, xo_ref, *norm_refs):
    a = a_ref[...]
    c = c_ref[...]
    _residual_out(x_ref, xo_ref, norm_refs, g_ref, 1.0,
                  lambda lo, hi: _dot(a, w_ref[0:A_Q, lo:hi]) + _dot(c, w_ref[A_Q:A_Q + B_CH, lo:hi]))


def _even_out(a, c, x, w_out, g_next):
    return _out_call(_even_out_kernel, "even_out", [a, c], x, w_out, [], g_next, True)


def _odd_in_c_kernel(xg_ref, inv_ref, w_ref, gup_ref, gb_ref, q_ref, k_ref, v_ref, og_ref, laf_ref, lab_ref):
    xg = xg_ref[...]
    inv = inv_ref[...]
    q_ref[...] = _scaled_dot(xg, w_ref[:, 0:C_QK], inv) * (C_DK ** -0.5)
    k_ref[...] = _scaled_dot(xg, w_ref[:, C_QK:2 * C_QK], inv)
    o = 2 * C_QK
    for c in range(C_V // FF_CHUNK):
        lo = c * FF_CHUNK
        v_ref[:, lo:lo + FF_CHUNK] = _scaled_dot(xg, w_ref[:, o + lo:o + lo + FF_CHUNK], inv).astype(BF16)
        og_ref[:, lo:lo + FF_CHUNK] = _scaled_dot(
            xg, w_ref[:, o + C_V + lo:o + C_V + lo + FF_CHUNK], inv).astype(BF16)
    o = o + 2 * C_V
    lr = _scaled_dot(xg, w_ref[:, o:o + LR_PAD], inv).astype(BF16)
    pre = _dot(lr, gup_ref[...]) + gb_ref[...]
    loga = _log_sigmoid(pre) * (1.0 / C_TAU)
    laf_ref[...] = loga[:, 0:C_QK]
    lab_ref[...] = loga[:, C_QK:2 * C_QK]


def _odd_in_c(xn, w_c, gup, gbias):
    n, d = xn[0].shape
    qk = jax.ShapeDtypeStruct((n, C_QK), F32)
    vv = jax.ShapeDtypeStruct((n, C_V), BF16)
    return pl.pallas_call(
        _odd_in_c_kernel,
        out_shape=[qk, qk, vv, vv, qk, qk],
        grid=(n // TM,),
        in_specs=_norm_specs(d) + [_resident(w_c.shape), _resident(gup.shape), _resident(gbias.shape)],
        out_specs=[_rows(TM, C_QK), _rows(TM, C_QK), _rows(TM, C_V), _rows(TM, C_V),
                   _rows(TM, C_QK), _rows(TM, C_QK)],
        compiler_params=_params(
            ("parallel",),
            w_c.size * 2 + 2 * TM * (d * 2 + 4 * C_QK * 4 + 2 * C_V * 2) + 6 * TM * FF_CHUNK * 4),
        name="odd_in_c",
    )(*xn, w_c, gup, gbias)


def _odd_in_d_kernel(xg_ref, inv_ref, w_ref, xr_ref, yg_ref):
    xg = xg_ref[...]
    inv = inv_ref[...]
    for c in range(D_WIDTH // FF_CHUNK):
        lo = c * FF_CHUNK
        xr_ref[:, lo:lo + FF_CHUNK] = _scaled_dot(xg, w_ref[:, lo:lo + FF_CHUNK], inv)
        yg_ref[:, lo:lo + FF_CHUNK] = _scaled_dot(
            xg, w_ref[:, D_WIDTH + lo:D_WIDTH + lo + FF_CHUNK], inv).astype(BF16)


def _odd_in_d(xn, w_d):
    n, d = xn[0].shape
    return pl.pallas_call(
        _odd_in_d_kernel,
        out_shape=[jax.ShapeDtypeStruct((n, D_WIDTH), F32), jax.ShapeDtypeStruct((n, D_WIDTH), BF16)],
        grid=(n // TM,),
        in_specs=_norm_specs(d) + [_resident(w_d.shape)],
        out_specs=[_rows(TM, D_WIDTH), _rows(TM, D_WIDTH)],
        compiler_params=_params(("parallel",), w_d.size * 2 + 2 * TM * (d * 2 + D_WIDTH * 6) + 4 * TM * FF_CHUNK * 4),
        name="odd_in_d",
    )(*xn, w_d)


def _split3(x):
    hi = x.astype(BF16)
    r1 = x - hi.astype(F32)
    mid = r1.astype(BF16)
    lo = (r1 - mid.astype(F32)).astype(BF16)
    return hi, mid, lo


def _gla_prep(q, k, b, reverse):
    cc, sub = GLA_C, GLA_SUB
    nsub = cc // sub
    r128 = lax.broadcasted_iota(jnp.int32, (cc, C_DK), 0)
    zero_row = jnp.zeros((1, C_DK), F32)
    if reverse:
        refs = [b[(i + 1) * sub:(i + 1) * sub + 1] for i in range(nsub - 1)] + [zero_row]
        b_end = b[0:1]
    else:
        refs = [zero_row] + [b[i * sub - 1:i * sub] for i in range(1, nsub)]
        b_end = b[cc - 1:cc]
    ref_rows = jnp.concatenate([jnp.broadcast_to(r, (sub, C_DK)) for r in refs], axis=0)
    qt = q * jnp.exp(b - ref_rows)
    q_inter = (qt * jnp.exp(ref_rows)).astype(BF16)
    kvars = []
    for i in range(nsub):
        seen = (r128 >= i * sub) if reverse else (r128 < (i + 1) * sub)
        kvars.append((k * jnp.exp(jnp.where(seen, refs[i] - b, -1e30))).astype(BF16))
    k_out = (k * jnp.exp(b_end - b)).astype(BF16)
    return qt.astype(BF16), q_inter, jnp.concatenate(kvars, axis=0), k_out, jnp.exp(b_end)


def _gla_kernel(qf_ref, kf_ref, vf_ref, laf_ref, qb_ref, kb_ref, vb_ref, lab_ref, of_ref, ob_ref, sf_ref, sb_ref):
    @pl.when(pl.program_id(1) == 0)
    def _():
        sf_ref[...] = jnp.zeros_like(sf_ref)
        sb_ref[...] = jnp.zeros_like(sb_ref)

    cc, sub, tt = GLA_C, GLA_SUB, GLA_T
    nch = tt // cc
    dirs = ((qf_ref, kf_ref, vf_ref, laf_ref, of_ref, sf_ref, False),
            (qb_ref, kb_ref, vb_ref, lab_ref, ob_ref, sb_ref, True))
    units = [(d, c, h) for d in range(2) for c in range(nch) for h in range(C_HEADS)]

    row = lax.broadcasted_iota(jnp.int32, (tt, tt), 0)
    col = lax.broadcasted_iota(jnp.int32, (tt, tt), 1)
    same_chunk = (row // cc) == (col // cc)
    b_tile = []
    for (_, _, _, la_ref, _, _, reverse) in dirs:
        tri = (same_chunk & ((col >= row) if reverse else (col <= row))).astype(BF16)
        hi, mid, lo = _split3(la_ref[...])
        b_tile.append(_dot(tri, hi) + _dot(tri, mid) + _dot(tri, lo))

    prep = {}
    for (d, c, h) in units:
        q_ref, k_ref, _, _, _, _, reverse = dirs[d]
        rows = slice(c * cc, (c + 1) * cc)
        cols = slice(h * C_DK, (h + 1) * C_DK)
        prep[d, c, h] = _gla_prep(q_ref[rows, cols], k_ref[rows, cols], b_tile[d][rows, cols], reverse)

    scores = {u: _dot_nt(prep[u][0], prep[u][2]) for u in units}

    trow = lax.broadcasted_iota(jnp.int32, (cc, V7X_LANES), 0)
    lane = lax.broadcasted_iota(jnp.int32, (cc, V7X_LANES), 1)
    own_half = (lane // cc) == ((trow // sub) % 2)
    first_pair = (trow // sub) < 2
    att = {}
    for u in units:
        reverse = dirs[u[0]][6]
        key = lane % cc
        keep = own_half & ((key >= trow) if reverse else (key <= trow))
        r = scores[u]
        att[u] = jnp.where(keep, jnp.where(first_pair, r[:, 0:V7X_LANES], r[:, V7X_LANES:2 * V7X_LANES]), 0.0).astype(BF16)

    o_intra, kv = {}, {}
    for (d, c, h) in units:
        v = dirs[d][2][c * cc:(c + 1) * cc, h * C_DV:(h + 1) * C_DV]
        o_intra[d, c, h] = _dot(att[d, c, h], jnp.concatenate([v, v], axis=0))
        kv[d, c, h] = _dot_tn(v, prep[d, c, h][3])

    state = {(d, h): dirs[d][5][h] for d in range(2) for h in range(C_HEADS)}
    for step in range(nch):
        for d in range(2):
            c = (nch - 1 - step) if dirs[d][6] else step
            outs = []
            for h in range(C_HEADS):
                st = state[d, h]
                outs.append(o_intra[d, c, h] + _dot_nt(prep[d, c, h][1], st.astype(BF16)))
                state[d, h] = prep[d, c, h][4] * st + kv[d, c, h]
            dirs[d][4][c * cc:(c + 1) * cc, :] = jnp.concatenate(outs, axis=1).astype(BF16)
    for d in range(2):
        for h in range(C_HEADS):
            dirs[d][5][h] = state[d, h]


def _gla(q, k, v, la_f, la_b, bsz, seq):
    n = bsz * seq
    nt = seq // GLA_T

    def fwd(width):
        return pl.BlockSpec((GLA_T, width), lambda b, i: (b * nt + i, 0))

    def bwd(width):
        return pl.BlockSpec((GLA_T, width), lambda b, i: (b * nt + nt - 1 - i, 0))

    o = jax.ShapeDtypeStruct((n, C_V), BF16)
    st = pltpu.VMEM((C_HEADS, C_DV, C_DK), F32)
    return pl.pallas_call(
        _gla_kernel,
        out_shape=[o, o],
        grid=(bsz, nt),
        in_specs=[fwd(C_QK), fwd(C_QK), fwd(C_V), fwd(C_QK), bwd(C_QK), bwd(C_QK), bwd(C_V), bwd(C_QK)],
        out_specs=[fwd(C_V), bwd(C_V)],
        scratch_shapes=[st, st],
        compiler_params=_params(("parallel", "arbitrary"), 4 * GLA_T * (3 * C_QK * 4 + 2 * C_V * 2) + (8 << 20)),
        name="gla",
    )(q, k, v, la_f, q, k, v, la_b)


def _sublane_scan(a, u, reverse):
    t = a.shape[0]
    r8 = lax.broadcasted_iota(jnp.int32, a.shape, 0) & (V7X_SUBLANES - 1)
    d = 1
    while d < V7X_SUBLANES:
        if reverse:
            ush, ash, ok = pltpu.roll(u, t - d, 0), pltpu.roll(a, t - d, 0), r8 < V7X_SUBLANES - d
        else:
            ush, ash, ok = pltpu.roll(u, d, 0), pltpu.roll(a, d, 0), r8 >= d
        u = u + a * jnp.where(ok, ush, 0.0)
        a = a * jnp.where(ok, ash, 1.0)
        d *= 2
    return a, u


def _rglru_dir(xp_ref, hs_ref, cw_ref, cb_ref, w_ref, ba_ref, bx_ref, lam_ref, o_ref, carry_ref, reverse):
    seg = RG_SEG
    nm = CONV_T // seg
    nv = nm // V7X_SUBLANES
    first = V7X_SUBLANES - (D_CONV - 1) // 2
    r8 = lax.broadcasted_iota(jnp.int32, (V7X_SUBLANES, D_BLOCK_DIM), 0)
    tiny = float(jnp.finfo(F32).tiny)
    for blk in range(D_BLOCKS):
        cols = slice(blk * D_BLOCK_DIM, (blk + 1) * D_BLOCK_DIM)
        xcs = []
        for k in range(seg):
            xc = cb_ref[:, cols]
            for j in range(D_CONV):
                xc = xc + xp_ref[blk, pl.ds(first + k + j, nm, stride=seg), :] * cw_ref[j:j + 1, cols]
            xcs.append(xc)
        xc = jnp.concatenate(xcs, axis=0)
        pr = _dot(xc.astype(BF16), w_ref[blk])
        r = jax.nn.sigmoid(pr[:, 0:D_BLOCK_DIM] + ba_ref[:, cols])
        gate_i = jax.nn.sigmoid(pr[:, D_BLOCK_DIM:] + bx_ref[:, cols])
        log_a = (D_C * r) * _log_sigmoid(lam_ref[:, cols])
        a = jnp.exp(log_a)
        th = jnp.tanh(log_a)
        y = -2.0 * th / (1.0 - th)
        u = (y * lax.rsqrt(jnp.maximum(y, tiny))) * (gate_i * xc)
        ak = [a[k * nm:(k + 1) * nm] for k in range(seg)]
        uk = [u[k * nm:(k + 1) * nm] for k in range(seg)]
        for k in (range(seg - 2, -1, -1) if reverse else range(1, seg)):
            kp = k + 1 if reverse else k - 1
            uk[k] = uk[k] + ak[k] * uk[kp]
            ak[k] = ak[k] * ak[kp]
        edge = 0 if reverse else seg - 1
        seg_a, seg_u = _sublane_scan(ak[edge], uk[edge], reverse)
        c = carry_ref[:, cols]
        cins = [None] * nv
        for g in (range(nv - 1, -1, -1) if reverse else range(nv)):
            rows = slice(g * V7X_SUBLANES, (g + 1) * V7X_SUBLANES)
            hend = seg_u[rows] + seg_a[rows] * c
            if reverse:
                cins[g] = jnp.where(r8 == V7X_SUBLANES - 1, c, pltpu.roll(hend, V7X_SUBLANES - 1, 0))
                c = hend[0:1]
            else:
                cins[g] = jnp.where(r8 == 0, c, pltpu.roll(hend, 1, 0))
                c = hend[V7X_SUBLANES - 1:V7X_SUBLANES]
        carry_ref[:, cols] = c
        cin = jnp.concatenate(cins, axis=0)
        for k in range(seg):
            hs_ref[blk, pl.ds(k, nm, stride=seg), :] = uk[k] + ak[k] * cin
        o_ref[:, cols] = hs_ref[blk].astype(BF16)


def _rglru_kernel(pf_ref, mf_ref, nf_ref, pb_ref, mb_ref, nb_ref, cw_ref, cb_ref, wf_ref, wb_ref,
                  ba_ref, bx_ref, lam_ref, hf_ref, hb_ref, xpf_ref, xpb_ref, hsf_ref, hsb_ref, cf_ref, cr_ref):
    i = pl.program_id(1)
    last = pl.num_programs(1) - 1

    @pl.when(i == 0)
    def _():
        cf_ref[...] = jnp.zeros_like(cf_ref)
        cr_ref[...] = jnp.zeros_like(cr_ref)

    _fill_padded_blocks(xpf_ref, pf_ref, mf_ref, nf_ref, i > 0, i < last)
    _fill_padded_blocks(xpb_ref, pb_ref, mb_ref, nb_ref, i < last, i > 0)
    _rglru_dir(xpf_ref, hsf_ref, cw_ref, cb_ref, wf_ref, ba_ref.at[0], bx_ref.at[0], lam_ref.at[0], hf_ref, cf_ref, False)
    _rglru_dir(xpb_ref, hsb_ref, cw_ref, cb_ref, wb_ref, ba_ref.at[1], bx_ref.at[1], lam_ref.at[1], hb_ref, cr_ref, True)


def _rglru(xr, conv_w, conv_b, w_f, w_b, ba, bx, lam, bsz, seq):
    n = bsz * seq
    nt = seq // CONV_T
    o = jax.ShapeDtypeStruct((n, D_WIDTH), BF16)
    vec2 = _resident((2, 1, D_WIDTH))
    carry = pltpu.VMEM((1, D_WIDTH), F32)
    padded = pltpu.VMEM((D_BLOCKS, CONV_T + 2 * V7X_SUBLANES, D_BLOCK_DIM), F32)
    slabs = pltpu.VMEM((D_BLOCKS, CONV_T, D_BLOCK_DIM), F32)
    return pl.pallas_call(
        _rglru_kernel,
        out_shape=[o, o],
        grid=(bsz, nt),
        in_specs=(_halo_specs(D_WIDTH, bsz, seq, CONV_T, V7X_SUBLANES, False)
                  + _halo_specs(D_WIDTH, bsz, seq, CONV_T, V7X_SUBLANES, True)
                  + [_resident((D_CONV, D_WIDTH)), _resident((1, D_WIDTH)), _resident(w_f.shape),
                     _resident(w_b.shape), vec2, vec2, vec2]),
        out_specs=[pl.BlockSpec((CONV_T, D_WIDTH), lambda b, i: (b * nt + i, 0)),
                   pl.BlockSpec((CONV_T, D_WIDTH), lambda b, i: (b * nt + nt - 1 - i, 0))],
        scratch_shapes=[padded, padded, slabs, slabs, carry, carry],
        compiler_params=_params(("parallel", "arbitrary"), 12 * CONV_T * D_WIDTH * 4 + (8 << 20)),
        name="rglru",
    )(xr, xr, xr, xr, xr, xr, conv_w, conv_b, w_f, w_b, ba, bx, lam)


def _odd_out_kernel(of_ref, ob_ref, og_ref, hf_ref, hb_ref, yg_ref, x_ref, w_ref, cg_ref, g_ref, xo_ref, *norm_refs):
    c_parts = []
    for h in range(C_HEADS):
        cols = slice(h * C_DV, (h + 1) * C_DV)
        o = of_ref[:, cols].astype(F32) + ob_ref[:, cols].astype(F32)
        og = og_ref[:, cols].astype(F32)
        c_parts.append((_rms(o, cg_ref[...]) * (og * jax.nn.sigmoid(og))).astype(BF16))
    c_out = jnp.concatenate(c_parts, axis=1)
    hsum = hf_ref[...].astype(F32) + hb_ref[...].astype(F32)
    d_out = (hsum * jax.nn.gelu(yg_ref[...].astype(F32))).astype(BF16)
    _residual_out(x_ref, xo_ref, norm_refs, g_ref, 1.0,
                  lambda lo, hi: _dot(c_out, w_ref[0:C_V, lo:hi]) + _dot(d_out, w_ref[C_V:C_V + D_WIDTH, lo:hi]))


def _odd_out(o_f, o_b, og, h_f, h_b, yg, x, w_out, c_norm_g, g_next):
    return _out_call(_odd_out_kernel, "odd_out", [o_f, o_b, og, h_f, h_b, yg], x, w_out, [c_norm_g], g_next, True)


def _prepare(p):
    row = lambda v: v.reshape(1, -1).astype(F32)
    stacks = {name: p[name].astype(BF16) for name in
              ("ffn1_w_in", "ffn1_w_out", "ffn2_w_in", "ffn2_w_out", "ev_w_in", "ev_w_out", "od_w_out")}
    layers = []
    for layer in range(DEPTH):
        j = layer // 2
        lw = {
            "ln_ffn1": row(p["ln_ffn1"][layer]), "ln_mix": row(p["ln_mix"][layer]), "ln_ffn2": row(p["ln_ffn2"][layer]),
            "ffn1_w_in": (stacks["ffn1_w_in"], layer), "ffn1_w_out": (stacks["ffn1_w_out"], layer),
            "ffn2_w_in": (stacks["ffn2_w_in"], layer), "ffn2_w_out": (stacks["ffn2_w_out"], layer),
        }
        if layer % 2 == 0:
            lw.update({
                "w_in": (stacks["ev_w_in"], j), "w_out": (stacks["ev_w_out"], j),
                "q_gain": row(p["a_q_gain"][j]), "k_gain": row(p["a_k_gain"][j]), "sink": p["a_sink"][j].astype(F32),
                "conv_w": p["b_conv_w"][j].astype(F32), "conv_b": row(p["b_conv_b"][j]),
                "norm_g": row(p["b_norm_g"][j]), "norm_b": row(p["b_norm_b"][j]),
            })
        else:
            w = p["od_w_in"][j]
            o4 = 2 * C_QK + 2 * C_V
            o5 = o4 + 2 * C_RANK
            w_c = jnp.concatenate([w[:, :o4], w[:, o4:o5], jnp.zeros((D_MODEL, LR_PAD - 2 * C_RANK), w.dtype)], axis=1)
            gup = jnp.zeros((LR_PAD, 2 * C_QK), F32)
            gup = gup.at[0:C_RANK, 0:C_QK].set(p["c_gate_up"][j, 0])
            gup = gup.at[C_RANK:2 * C_RANK, C_QK:2 * C_QK].set(p["c_gate_up"][j, 1])
            lw.update({
                "w_c": w_c.astype(BF16), "w_d": w[:, o5:].astype(BF16), "w_out": (stacks["od_w_out"], j),
                "gup": gup.astype(BF16), "gbias": p["c_gate_bias"][j].reshape(1, 2 * C_QK).astype(F32),
                "c_norm_g": row(p["c_norm_g"][j]),
                "conv_w": p["d_conv_w"][j].astype(F32), "conv_b": row(p["d_conv_b"][j]),
                "w_f": jnp.concatenate([p["d_wa"][j, 0], p["d_wx"][j, 0]], axis=-1).astype(BF16),
                "w_b": jnp.concatenate([p["d_wa"][j, 1], p["d_wx"][j, 1]], axis=-1).astype(BF16),
                "ba": p["d_ba"][j].reshape(2, 1, D_WIDTH).astype(F32), "bx": p["d_bx"][j].reshape(2, 1, D_WIDTH).astype(F32),
                "lam": p["d_lambda"][j].reshape(2, 1, D_WIDTH).astype(F32),
            })
        layers.append(lw)
    return layers


def _run_trunk(x3, layers):
    bsz, seq, d = x3.shape
    x = x3.reshape(bsz * seq, d)
    xn = _norm(x, layers[0]["ln_ffn1"])
    for layer, lw in enumerate(layers):
        act = _ffn_in(xn, lw["ffn1_w_in"])
        x, xn = _ffn_out(act, x, lw["ffn1_w_out"], lw["ln_mix"], True)
        if layer % 2 == 0:
            q, k, v, glu = _even_in(xn, lw["w_in"], lw["q_gain"], lw["k_gain"])
            a = _attention(q, k, v, lw["sink"], bsz, seq)
            c = _conv_b(glu, lw["conv_w"], lw["conv_b"], lw["norm_g"], lw["norm_b"], bsz, seq)
            x, xn = _even_out(a, c, x, lw["w_out"], lw["ln_ffn2"])
        else:
            q, k, v, og, la_f, la_b = _odd_in_c(xn, lw["w_c"], lw["gup"], lw["gbias"])
            xr, yg = _odd_in_d(xn, lw["w_d"])
            o_f, o_b = _gla(q, k, v, la_f, la_b, bsz, seq)
            h_f, h_b = _rglru(xr, lw["conv_w"], lw["conv_b"], lw["w_f"], lw["w_b"], lw["ba"], lw["bx"], lw["lam"],
                              bsz, seq)
            x, xn = _odd_out(o_f, o_b, og, h_f, h_b, yg, x, lw["w_out"], lw["c_norm_g"], lw["ln_ffn2"])
        act = _ffn_in(xn, lw["ffn2_w_in"])
        last = layer == len(layers) - 1
        g_next = lw["ln_ffn2"] if last else layers[layer + 1]["ln_ffn1"]
        x, xn = _ffn_out(act, x, lw["ffn2_w_out"], g_next, not last)
    return x.reshape(bsz, seq, d)


def kernel(x_prompt, x_sample, ln_ffn1, ffn1_w_in, ffn1_w_out, ln_mix, ln_ffn2, ffn2_w_in, ffn2_w_out, ev_w_in, ev_w_out, a_q_gain, a_k_gain, a_sink, b_conv_w, b_conv_b, b_norm_g, b_norm_b, od_w_in, od_w_out, c_gate_up, c_gate_bias, c_norm_g, d_conv_w, d_conv_b, d_wa, d_ba, d_wx, d_bx, d_lambda):
    layers = _prepare(dict(
        ln_ffn1=ln_ffn1, ffn1_w_in=ffn1_w_in, ffn1_w_out=ffn1_w_out, ln_mix=ln_mix, ln_ffn2=ln_ffn2,
        ffn2_w_in=ffn2_w_in, ffn2_w_out=ffn2_w_out, ev_w_in=ev_w_in, ev_w_out=ev_w_out, a_q_gain=a_q_gain,
        a_k_gain=a_k_gain, a_sink=a_sink, b_conv_w=b_conv_w, b_conv_b=b_conv_b, b_norm_g=b_norm_g,
        b_norm_b=b_norm_b, od_w_in=od_w_in, od_w_out=od_w_out, c_gate_up=c_gate_up, c_gate_bias=c_gate_bias,
        c_norm_g=c_norm_g, d_conv_w=d_conv_w, d_conv_b=d_conv_b, d_wa=d_wa, d_ba=d_ba, d_wx=d_wx, d_bx=d_bx,
        d_lambda=d_lambda))
    return (_run_trunk(x_prompt, layers), _run_trunk(x_sample, layers))
```

```python
import functools

import jax
import jax.numpy as jnp
from jax import lax
from jax.experimental import pallas as pl
from jax.experimental.pallas import tpu as pltpu

F32 = jnp.float32
BF16 = jnp.bfloat16

D_MODEL = 4096
DEPTH = 4
D_FF = 3 * D_MODEL // 8
EPS = 1e-6
MIX_HALF = D_MODEL // 4
A_HEAD_DIM = 128
A_HEADS = MIX_HALF // A_HEAD_DIM
A_KV_HEADS = 2
A_GROUP = A_HEADS // A_KV_HEADS
A_WINDOW = 128
A_BLOCK = 128
A_Q = A_HEADS * A_HEAD_DIM
A_KV = A_KV_HEADS * A_HEAD_DIM
B_CH = MIX_HALF
B_CONV = 31
C_HEADS = 4
C_DV = MIX_HALF // C_HEADS
C_DK = C_DV // 2
C_RANK = 16
C_TAU = 16.0
C_QK = C_HEADS * C_DK
C_V = C_HEADS * C_DV
D_WIDTH = MIX_HALF
D_BLOCKS = 8
D_BLOCK_DIM = D_WIDTH // D_BLOCKS
D_CONV = 4
D_C = 8.0

V7X_LANES = 128
V7X_SUBLANES = 8
V7X_VMEM_BYTES = 64 * 1024 * 1024

TM = 256
FFN_PHASES = 1
FF_CHUNK = 512
OUT_CHUNK = 1024
CONV_T = 256
CONV_HALO = 16
CONV_RC = 128
CONV_STRIDE = 2
RG_SEG = 4
GLA_T = 256
GLA_C = 64
GLA_SUB = 16
LR_PAD = 128


def _vmem_limit(nbytes):
    return int(min(nbytes * 5 // 4 + (4 << 20), V7X_VMEM_BYTES - (4 << 20)))


def _params(sem, nbytes):
    return pltpu.CompilerParams(dimension_semantics=sem, vmem_limit_bytes=_vmem_limit(nbytes))


def _resident(shape):
    nd = len(shape)
    return pl.BlockSpec(shape, lambda *_: (0,) * nd, pipeline_mode=pl.Buffered(1))


def _layer_spec(w):
    stack, layer = w
    nd = stack.ndim
    return pl.BlockSpec((None,) + stack.shape[1:], lambda *_: (layer,) + (0,) * (nd - 1),
                        pipeline_mode=pl.Buffered(1))


def _layer_bytes(w):
    stack = w[0]
    return stack.size // stack.shape[0] * stack.dtype.itemsize


def _rows(tm, width):
    return pl.BlockSpec((tm, width), lambda i: (i, 0))


def _dot(a, b):
    return jnp.dot(a, b, preferred_element_type=F32)


def _dot_nt(a, b):
    return lax.dot_general(a, b, (((1,), (1,)), ((), ())), preferred_element_type=F32)


def _dot_tn(a, b):
    return lax.dot_general(a, b, (((0,), (0,)), ((), ())), preferred_element_type=F32)


def _rms(x, g):
    inv = lax.rsqrt(jnp.mean(x * x, axis=-1, keepdims=True) + EPS)
    return (x * inv) * g


def _log_sigmoid(x):
    return jnp.minimum(x, 0.0) - jnp.log1p(jnp.exp(-jnp.abs(x)))


def _scaled_dot(xg, w, inv):
    return _dot(xg, w) * jnp.concatenate([inv] * (w.shape[1] // V7X_LANES), axis=1)


def _norm_specs(d):
    return [_rows(TM, d), _rows(TM, V7X_LANES)]


def _ffn_kernel(x_ref, g_ref, w_in_ref, w_out_ref, *rest, emit_norm):
    if emit_norm:
        gn_ref, xo_ref, xgo_ref, invo_ref, xg_s, act_s, ssq_s = rest
    else:
        xo_ref, xg_s, act_s = rest
    width = D_MODEL // FFN_PHASES
    ph = pl.program_id(1)

    def down_proj(k):
        act = act_s[...]
        ssq = jnp.zeros((x_ref.shape[0], 1), F32)
        for c in range(width // OUT_CHUNK):
            lo = k * width + c * OUT_CHUNK
            out_cols = slice(c * OUT_CHUNK, (c + 1) * OUT_CHUNK)
            xnew = x_ref[:, lo:lo + OUT_CHUNK] + 0.5 * _dot(act, w_out_ref[:, lo:lo + OUT_CHUNK])
            xo_ref[:, out_cols] = xnew
            if emit_norm:
                xgo_ref[:, out_cols] = (xnew * gn_ref[:, lo:lo + OUT_CHUNK]).astype(BF16)
                ssq = ssq + jnp.sum(xnew * xnew, axis=-1, keepdims=True)
        if emit_norm:
            if k > 0:
                ssq = ssq + ssq_s[:, 0:1]
            if k < FFN_PHASES - 1:
                ssq_s[...] = jnp.broadcast_to(ssq, ssq_s.shape)
            else:
                inv = lax.rsqrt(ssq * (1.0 / D_MODEL) + EPS)
                invo_ref[...] = jnp.broadcast_to(inv, invo_ref.shape)

    @pl.when(ph == 0)
    def _():
        x = x_ref[...]
        inv = lax.rsqrt(jnp.mean(x * x, axis=-1, keepdims=True) + EPS)
        xg_s[...] = (x * g_ref[...]).astype(BF16)
        xg = xg_s[...]
        for c in range(D_FF // FF_CHUNK):
            lo = c * FF_CHUNK
            gate = _dot(xg, w_in_ref[:, lo:lo + FF_CHUNK]) * inv
            up = _dot(xg, w_in_ref[:, D_FF + lo:D_FF + lo + FF_CHUNK]) * inv
            act_s[:, lo:lo + FF_CHUNK] = (gate * jax.nn.sigmoid(gate) * up).astype(BF16)
        down_proj(0)

    for k in range(1, FFN_PHASES):
        pl.when(ph == k)(functools.partial(down_proj, k))


def _ffn(x, g, w_in, w_out, g_next=None):
    n, d = x.shape
    width = d // FFN_PHASES
    emit_norm = g_next is not None
    out_shape = [jax.ShapeDtypeStruct((n, d), F32)]
    out_specs = [pl.BlockSpec((TM, width), lambda i, p: (i, p))]
    scratch = [pltpu.VMEM((TM, d), BF16), pltpu.VMEM((TM, D_FF), BF16)]
    operands = [x, g, w_in[0], w_out[0]]
    in_specs = [pl.BlockSpec((TM, d), lambda i, p: (i, 0)), _resident((1, d)), _layer_spec(w_in), _layer_spec(w_out)]
    if emit_norm:
        operands.append(g_next)
        in_specs.append(_resident((1, d)))
        out_shape += [jax.ShapeDtypeStruct((n, d), BF16), jax.ShapeDtypeStruct((n, V7X_LANES), F32)]
        out_specs += [pl.BlockSpec((TM, width), lambda i, p: (i, p)),
                      pl.BlockSpec((TM, V7X_LANES), lambda i, p: (i, 0))]
        scratch.append(pltpu.VMEM((TM, V7X_LANES), F32))
    nbytes = (_layer_bytes(w_in) + _layer_bytes(w_out) + 2 * TM * d * 4 + 2 * TM * width * (4 + 2)
              + TM * (d + D_FF) * 2)
    res = pl.pallas_call(
        functools.partial(_ffn_kernel, emit_norm=emit_norm),
        out_shape=out_shape,
        grid=(n // TM, FFN_PHASES),
        in_specs=in_specs,
        out_specs=out_specs,
        scratch_shapes=scratch,
        compiler_params=_params(("parallel", "arbitrary"), nbytes),
        name="ffn",
    )(*operands)
    return (res[0], (res[1], res[2])) if emit_norm else (res[0], None)


def _residual_out(x_ref, xo_ref, branch_fn):
    for c in range(D_MODEL // OUT_CHUNK):
        lo = c * OUT_CHUNK
        xo_ref[:, lo:lo + OUT_CHUNK] = x_ref[:, lo:lo + OUT_CHUNK] + branch_fn(lo, lo + OUT_CHUNK)


def _out_call(kernel, name, row_inputs, x, w_out, vectors):
    n, d = x.shape
    nbytes = _layer_bytes(w_out)
    nbytes += 2 * sum(TM * a.shape[1] * a.dtype.itemsize for a in row_inputs)
    nbytes += 2 * TM * d * (4 + 4) + 2 * TM * OUT_CHUNK * 4
    return pl.pallas_call(
        kernel,
        out_shape=jax.ShapeDtypeStruct((n, d), F32),
        grid=(n // TM,),
        in_specs=([_rows(TM, a.shape[1]) for a in row_inputs] + [_rows(TM, d)]
                  + [_layer_spec(w_out)] + [_resident(v.shape) for v in vectors]),
        out_specs=_rows(TM, d),
        compiler_params=_params(("parallel",), nbytes),
        name=name,
    )(*row_inputs, x, w_out[0], *vectors)


def _head_norm(p, gain):
    inv = lax.rsqrt(jnp.mean(p * p, axis=-1, keepdims=True) + EPS)
    return ((p * inv) * gain).astype(BF16)


def _even_in_kernel(xg_ref, inv_ref, w_ref, qg_ref, kg_ref, q_ref, k_ref, v_ref, glu_ref):
    xg = xg_ref[...]
    inv = inv_ref[...]
    hd = A_HEAD_DIM
    for c in range(A_Q // FF_CHUNK):
        p = _scaled_dot(xg, w_ref[:, c * FF_CHUNK:(c + 1) * FF_CHUNK], inv)
        for h in range(FF_CHUNK // hd):
            q_ref[:, c * FF_CHUNK + h * hd:c * FF_CHUNK + (h + 1) * hd] = _head_norm(
                p[:, h * hd:(h + 1) * hd], qg_ref[...])
    p = _scaled_dot(xg, w_ref[:, A_Q:A_Q + 2 * A_KV], inv)
    for h in range(A_KV_HEADS):
        k_ref[:, h * hd:(h + 1) * hd] = _head_norm(p[:, h * hd:(h + 1) * hd], kg_ref[...])
    v_ref[...] = p[:, A_KV:].astype(BF16)
    u0 = A_Q + 2 * A_KV
    for c in range(B_CH // FF_CHUNK):
        lo = c * FF_CHUNK
        u = _scaled_dot(xg, w_ref[:, u0 + lo:u0 + lo + FF_CHUNK], inv)
        gate = _scaled_dot(xg, w_ref[:, u0 + B_CH + lo:u0 + B_CH + lo + FF_CHUNK], inv)
        glu_ref[:, lo:lo + FF_CHUNK] = u * jax.nn.sigmoid(gate)


def _even_in(xn, w_in, q_gain, k_gain):
    n, d = xn[0].shape
    return pl.pallas_call(
        _even_in_kernel,
        out_shape=[jax.ShapeDtypeStruct((n, A_Q), BF16), jax.ShapeDtypeStruct((n, A_KV), BF16),
                   jax.ShapeDtypeStruct((n, A_KV), BF16), jax.ShapeDtypeStruct((n, B_CH), F32)],
        grid=(n // TM,),
        in_specs=_norm_specs(d) + [_layer_spec(w_in), _resident((1, A_HEAD_DIM)),
                                   _resident((1, A_HEAD_DIM))],
        out_specs=[_rows(TM, A_Q), _rows(TM, A_KV), _rows(TM, A_KV), _rows(TM, B_CH)],
        compiler_params=_params(
            ("parallel",),
            _layer_bytes(w_in) + 2 * TM * (d * 2 + A_Q * 2 + A_KV * 4 + B_CH * 4) + 4 * TM * FF_CHUNK * 4),
        name="even_in",
    )(*xn, w_in[0], q_gain, k_gain)


def _attn_kernel(sink_ref, q_ref, k_ref, v_ref, o_ref, *, seq):
    blk, hd = A_BLOCK, A_HEAD_DIM
    span = 3 * blk
    n = pl.program_id(1)
    start = pl.multiple_of(jnp.clip((n - 1) * blk, 0, seq - span), blk)
    kb = k_ref[0, pl.ds(start, span), :]
    vb = v_ref[0, pl.ds(start, span), :]
    qpos = n * blk + lax.broadcasted_iota(jnp.int32, (blk, span), 0)
    kpos = start + lax.broadcasted_iota(jnp.int32, (blk, span), 1)
    absrel = jnp.abs(kpos - qpos).astype(F32)
    valid = absrel <= float(A_WINDOW)
    scale = hd ** -0.5
    for g in range(A_KV_HEADS):
        kg = kb[:, g * hd:(g + 1) * hd]
        vg = vb[:, g * hd:(g + 1) * hd]
        qg = jnp.concatenate(
            [q_ref[:, (g * A_GROUP + j) * hd:(g * A_GROUP + j + 1) * hd] for j in range(A_GROUP)], axis=0)
        s = _dot_nt(qg, kg) * scale
        probs = []
        for j in range(A_GROUP):
            h = g * A_GROUP + j
            slope = 2.0 ** (-8.0 * (h + 1) / A_HEADS)
            sj = jnp.where(valid, s[j * blk:(j + 1) * blk] - slope * absrel, -jnp.inf)
            sink = sink_ref[h]
            m = jnp.maximum(jnp.max(sj, axis=-1, keepdims=True), sink)
            p = jnp.exp(sj - m)
            den = jnp.sum(p, axis=-1, keepdims=True) + jnp.exp(sink - m)
            probs.append((p * (1.0 / den)).astype(BF16))
        og = _dot(jnp.concatenate(probs, axis=0), vg)
        for j in range(A_GROUP):
            h = g * A_GROUP + j
            o_ref[:, h * hd:(h + 1) * hd] = og[j * blk:(j + 1) * blk].astype(BF16)


def _attention(q, k, v, sink, bsz, seq):
    n = bsz * seq
    nb = seq // A_BLOCK
    k3 = k.reshape(bsz, seq, A_KV)
    v3 = v.reshape(bsz, seq, A_KV)
    return pl.pallas_call(
        functools.partial(_attn_kernel, seq=seq),
        out_shape=jax.ShapeDtypeStruct((n, A_Q), BF16),
        grid=(bsz, nb),
        in_specs=[pl.BlockSpec(memory_space=pltpu.SMEM),
                  pl.BlockSpec((A_BLOCK, A_Q), lambda b, i: (b * nb + i, 0)),
                  pl.BlockSpec((1, seq, A_KV), lambda b, i: (b, 0, 0)),
                  pl.BlockSpec((1, seq, A_KV), lambda b, i: (b, 0, 0))],
        out_specs=pl.BlockSpec((A_BLOCK, A_Q), lambda b, i: (b * nb + i, 0)),
        compiler_params=_params(("parallel", "arbitrary"),
                                4 * seq * A_KV * 2 + 4 * A_BLOCK * A_Q * 2 + 16 * A_BLOCK * 3 * A_BLOCK * 4),
        name="window_attn",
    )(sink, q, k3, v3)


def _halo_specs(width, bsz, seq, tile, halo, reverse):
    nt = seq // tile
    per_tile = tile // halo
    per_seq = seq // halo

    def t_of(i):
        return (nt - 1 - i) if reverse else i

    def prev_map(b, i):
        t = t_of(i)
        return (b * per_seq + jnp.maximum(t * per_tile - 1, 0), 0)

    def main_map(b, i):
        return (b * nt + t_of(i), 0)

    def next_map(b, i):
        t = t_of(i)
        return (b * per_seq + jnp.minimum((t + 1) * per_tile, per_seq - 1), 0)

    return [pl.BlockSpec((halo, width), prev_map), pl.BlockSpec((tile, width), main_map),
            pl.BlockSpec((halo, width), next_map)]


def _fill_padded_blocks(xp_ref, prev_ref, main_ref, next_ref, has_prev, has_next):
    halo = prev_ref.shape[0]
    tile = main_ref.shape[0]
    for s in range(xp_ref.shape[0]):
        cols = slice(s * V7X_LANES, (s + 1) * V7X_LANES)
        xp_ref[s, 0:halo, :] = jnp.where(has_prev, prev_ref[:, cols], 0.0)
        xp_ref[s, halo:halo + tile, :] = main_ref[:, cols]
        xp_ref[s, halo + tile:halo + tile + halo, :] = jnp.where(has_next, next_ref[:, cols], 0.0)


def _convb_kernel(prev_ref, main_ref, next_ref, w_ref, b_ref, g_ref, beta_ref, o_ref, xp_ref, y_ref):
    i = pl.program_id(1)
    _fill_padded_blocks(xp_ref, prev_ref, main_ref, next_ref, i > 0, i < pl.num_programs(1) - 1)
    first = CONV_HALO - (B_CONV - 1) // 2
    nrows = CONV_RC // CONV_STRIDE
    groups = nrows // V7X_SUBLANES
    phases = [(r0, p) for r0 in range(0, CONV_T, CONV_RC) for p in range(CONV_STRIDE)]
    for s in range(B_CH // V7X_LANES):
        cols = slice(s * V7X_LANES, (s + 1) * V7X_LANES)
        accs = [None] * len(phases)
        for j in range(B_CONV):
            wj = jnp.broadcast_to(w_ref[j:j + 1, cols], (V7X_SUBLANES, V7X_LANES))[None]
            for idx, (r0, p) in enumerate(phases):
                x = xp_ref[s, pl.ds(r0 + p + first + j, nrows, stride=CONV_STRIDE), :]
                term = x.reshape(groups, V7X_SUBLANES, V7X_LANES) * wj
                accs[idx] = term if accs[idx] is None else accs[idx] + term
        for idx, (r0, p) in enumerate(phases):
            y_ref[s, pl.ds(r0 + p, nrows, stride=CONV_STRIDE), :] = (
                accs[idx].reshape(nrows, V7X_LANES) + b_ref[:, cols])
    y = jnp.concatenate([y_ref[s] for s in range(B_CH // V7X_LANES)], axis=1)
    mu = jnp.mean(y, axis=-1, keepdims=True)
    yc = y - mu
    var = jnp.mean(yc * yc, axis=-1, keepdims=True)
    yn = (yc * lax.rsqrt(var + EPS)) * g_ref[...] + beta_ref[...]
    o_ref[...] = (yn * jax.nn.sigmoid(yn)).astype(BF16)


def _conv_b(glu, conv_w, conv_b, norm_g, norm_b, bsz, seq):
    n = bsz * seq
    nt = seq // CONV_T
    vec = _resident((1, B_CH))
    return pl.pallas_call(
        _convb_kernel,
        out_shape=jax.ShapeDtypeStruct((n, B_CH), BF16),
        grid=(bsz, nt),
        in_specs=_halo_specs(B_CH, bsz, seq, CONV_T, CONV_HALO, False) + [_resident((B_CONV, B_CH)), vec, vec, vec],
        out_specs=pl.BlockSpec((CONV_T, B_CH), lambda b, i: (b * nt + i, 0)),
        scratch_shapes=[pltpu.VMEM((B_CH // V7X_LANES, CONV_T + 2 * CONV_HALO, V7X_LANES), F32),
                        pltpu.VMEM((B_CH // V7X_LANES, CONV_T, V7X_LANES), F32)],
        compiler_params=_params(("parallel", "arbitrary"), 8 * CONV_T * B_CH * 4),
        name="conformer_conv",
    )(glu, glu, glu, conv_w, conv_b, norm_g, norm_b)


def _even_out_kernel(a_ref, c_ref, x_ref, w_ref, xo_ref):
    a = a_ref[...]
    c = c_ref[...]
    _residual_out(x_ref, xo_ref,
                  lambda lo, hi: _dot(a, w_ref[0:A_Q, lo:hi]) + _dot(c, w_ref[A_Q:A_Q + B_CH, lo:hi]))


def _even_out(a, c, x, w_out):
    return _out_call(_even_out_kernel, "even_out", [a, c], x, w_out, [])


def _odd_in_c_kernel(xg_ref, inv_ref, w_ref, gup_ref, gb_ref, q_ref, k_ref, v_ref, og_ref, laf_ref, lab_ref):
    xg = xg_ref[...]
    inv = inv_ref[...]
    q_ref[...] = _scaled_dot(xg, w_ref[:, 0:C_QK], inv) * (C_DK ** -0.5)
    k_ref[...] = _scaled_dot(xg, w_ref[:, C_QK:2 * C_QK], inv)
    o = 2 * C_QK
    for c in range(C_V // FF_CHUNK):
        lo = c * FF_CHUNK
        v_ref[:, lo:lo + FF_CHUNK] = _scaled_dot(xg, w_ref[:, o + lo:o + lo + FF_CHUNK], inv).astype(BF16)
        og_ref[:, lo:lo + FF_CHUNK] = _scaled_dot(
            xg, w_ref[:, o + C_V + lo:o + C_V + lo + FF_CHUNK], inv).astype(BF16)
    o = o + 2 * C_V
    lr = _scaled_dot(xg, w_ref[:, o:o + LR_PAD], inv).astype(BF16)
    pre = _dot(lr, gup_ref[...]) + gb_ref[...]
    loga = _log_sigmoid(pre) * (1.0 / C_TAU)
    laf_ref[...] = loga[:, 0:C_QK]
    lab_ref[...] = loga[:, C_QK:2 * C_QK]


def _odd_in_c(xn, w_c, gup, gbias):
    n, d = xn[0].shape
    qk = jax.ShapeDtypeStruct((n, C_QK), F32)
    vv = jax.ShapeDtypeStruct((n, C_V), BF16)
    return pl.pallas_call(
        _odd_in_c_kernel,
        out_shape=[qk, qk, vv, vv, qk, qk],
        grid=(n // TM,),
        in_specs=_norm_specs(d) + [_resident(w_c.shape), _resident(gup.shape), _resident(gbias.shape)],
        out_specs=[_rows(TM, C_QK), _rows(TM, C_QK), _rows(TM, C_V), _rows(TM, C_V),
                   _rows(TM, C_QK), _rows(TM, C_QK)],
        compiler_params=_params(
            ("parallel",),
            w_c.size * 2 + 2 * TM * (d * 2 + 4 * C_QK * 4 + 2 * C_V * 2) + 6 * TM * FF_CHUNK * 4),
        name="odd_in_c",
    )(*xn, w_c, gup, gbias)


def _odd_in_d_kernel(xg_ref, inv_ref, w_ref, xr_ref, yg_ref):
    xg = xg_ref[...]
    inv = inv_ref[...]
    for c in range(D_WIDTH // FF_CHUNK):
        lo = c * FF_CHUNK
        xr_ref[:, lo:lo + FF_CHUNK] = _scaled_dot(xg, w_ref[:, lo:lo + FF_CHUNK], inv)
        yg_ref[:, lo:lo + FF_CHUNK] = _scaled_dot(
            xg, w_ref[:, D_WIDTH + lo:D_WIDTH + lo + FF_CHUNK], inv).astype(BF16)


def _odd_in_d(xn, w_d):
    n, d = xn[0].shape
    return pl.pallas_call(
        _odd_in_d_kernel,
        out_shape=[jax.ShapeDtypeStruct((n, D_WIDTH), F32), jax.ShapeDtypeStruct((n, D_WIDTH), BF16)],
        grid=(n // TM,),
        in_specs=_norm_specs(d) + [_resident(w_d.shape)],
        out_specs=[_rows(TM, D_WIDTH), _rows(TM, D_WIDTH)],
        compiler_params=_params(("parallel",), w_d.size * 2 + 2 * TM * (d * 2 + D_WIDTH * 6) + 4 * TM * FF_CHUNK * 4),
        name="odd_in_d",
    )(*xn, w_d)


def _split3(x):
    hi = x.astype(BF16)
    r1 = x - hi.astype(F32)
    mid = r1.astype(BF16)
    lo = (r1 - mid.astype(F32)).astype(BF16)
    return hi, mid, lo


def _gla_prep(q, k, b, reverse):
    cc, sub = GLA_C, GLA_SUB
    nsub = cc // sub
    r128 = lax.broadcasted_iota(jnp.int32, (cc, C_DK), 0)
    zero_row = jnp.zeros((1, C_DK), F32)
    if reverse:
        refs = [b[(i + 1) * sub:(i + 1) * sub + 1] for i in range(nsub - 1)] + [zero_row]
        b_end = b[0:1]
    else:
        refs = [zero_row] + [b[i * sub - 1:i * sub] for i in range(1, nsub)]
        b_end = b[cc - 1:cc]
    ref_rows = jnp.concatenate([jnp.broadcast_to(r, (sub, C_DK)) for r in refs], axis=0)
    qt = q * jnp.exp(b - ref_rows)
    q_inter = (qt * jnp.exp(ref_rows)).astype(BF16)
    kvars = []
    for i in range(nsub):
        seen = (r128 >= i * sub) if reverse else (r128 < (i + 1) * sub)
        kvars.append((k * jnp.exp(jnp.where(seen, refs[i] - b, -1e30))).astype(BF16))
    k_out = (k * jnp.exp(b_end - b)).astype(BF16)
    return qt.astype(BF16), q_inter, jnp.concatenate(kvars, axis=0), k_out, jnp.exp(b_end)


def _gla_kernel(qf_ref, kf_ref, vf_ref, laf_ref, qb_ref, kb_ref, vb_ref, lab_ref, of_ref, ob_ref, sf_ref, sb_ref):
    @pl.when(pl.program_id(1) == 0)
    def _():
        sf_ref[...] = jnp.zeros_like(sf_ref)
        sb_ref[...] = jnp.zeros_like(sb_ref)

    cc, sub, tt = GLA_C, GLA_SUB, GLA_T
    nch = tt // cc
    dirs = ((qf_ref, kf_ref, vf_ref, laf_ref, of_ref, sf_ref, False),
            (qb_ref, kb_ref, vb_ref, lab_ref, ob_ref, sb_ref, True))
    units = [(d, c, h) for d in range(2) for c in range(nch) for h in range(C_HEADS)]

    row = lax.broadcasted_iota(jnp.int32, (tt, tt), 0)
    col = lax.broadcasted_iota(jnp.int32, (tt, tt), 1)
    same_chunk = (row // cc) == (col // cc)
    b_tile = []
    for (_, _, _, la_ref, _, _, reverse) in dirs:
        tri = (same_chunk & ((col >= row) if reverse else (col <= row))).astype(BF16)
        hi, mid, lo = _split3(la_ref[...])
        b_tile.append(_dot(tri, hi) + _dot(tri, mid) + _dot(tri, lo))

    prep = {}
    for (d, c, h) in units:
        q_ref, k_ref, _, _, _, _, reverse = dirs[d]
        rows = slice(c * cc, (c + 1) * cc)
        cols = slice(h * C_DK, (h + 1) * C_DK)
        prep[d, c, h] = _gla_prep(q_ref[rows, cols], k_ref[rows, cols], b_tile[d][rows, cols], reverse)

    scores = {u: _dot_nt(prep[u][0], prep[u][2]) for u in units}

    trow = lax.broadcasted_iota(jnp.int32, (cc, V7X_LANES), 0)
    lane = lax.broadcasted_iota(jnp.int32, (cc, V7X_LANES), 1)
    own_half = (lane // cc) == ((trow // sub) % 2)
    first_pair = (trow // sub) < 2
    att = {}
    for u in units:
        reverse = dirs[u[0]][6]
        key = lane % cc
        keep = own_half & ((key >= trow) if reverse else (key <= trow))
        r = scores[u]
        att[u] = jnp.where(keep, jnp.where(first_pair, r[:, 0:V7X_LANES], r[:, V7X_LANES:2 * V7X_LANES]), 0.0).astype(BF16)

    o_intra, kv = {}, {}
    for (d, c, h) in units:
        v = dirs[d][2][c * cc:(c + 1) * cc, h * C_DV:(h + 1) * C_DV]
        o_intra[d, c, h] = _dot(att[d, c, h], jnp.concatenate([v, v], axis=0))
        kv[d, c, h] = _dot_tn(v, prep[d, c, h][3])

    state = {(d, h): dirs[d][5][h] for d in range(2) for h in range(C_HEADS)}
    for step in range(nch):
        for d in range(2):
            c = (nch - 1 - step) if dirs[d][6] else step
            outs = []
            for h in range(C_HEADS):
                st = state[d, h]
                outs.append(o_intra[d, c, h] + _dot_nt(prep[d, c, h][1], st.astype(BF16)))
                state[d, h] = prep[d, c, h][4] * st + kv[d, c, h]
            dirs[d][4][c * cc:(c + 1) * cc, :] = jnp.concatenate(outs, axis=1).astype(BF16)
    for d in range(2):
        for h in range(C_HEADS):
            dirs[d][5][h] = state[d, h]


def _gla(q, k, v, la_f, la_b, bsz, seq):
    n = bsz * seq
    nt = seq // GLA_T

    def fwd(width):
        return pl.BlockSpec((GLA_T, width), lambda b, i: (b * nt + i, 0))

    def bwd(width):
        return pl.BlockSpec((GLA_T, width), lambda b, i: (b * nt + nt - 1 - i, 0))

    o = jax.ShapeDtypeStruct((n, C_V), BF16)
    st = pltpu.VMEM((C_HEADS, C_DV, C_DK), F32)
    return pl.pallas_call(
        _gla_kernel,
        out_shape=[o, o],
        grid=(bsz, nt),
        in_specs=[fwd(C_QK), fwd(C_QK), fwd(C_V), fwd(C_QK), bwd(C_QK), bwd(C_QK), bwd(C_V), bwd(C_QK)],
        out_specs=[fwd(C_V), bwd(C_V)],
        scratch_shapes=[st, st],
        compiler_params=_params(("parallel", "arbitrary"), 4 * GLA_T * (3 * C_QK * 4 + 2 * C_V * 2) + (8 << 20)),
        name="gla",
    )(q, k, v, la_f, q, k, v, la_b)


def _sublane_scan(a, u, reverse):
    t = a.shape[0]
    r8 = lax.broadcasted_iota(jnp.int32, a.shape, 0) & (V7X_SUBLANES - 1)
    d = 1
    while d < V7X_SUBLANES:
        if reverse:
            ush, ash, ok = pltpu.roll(u, t - d, 0), pltpu.roll(a, t - d, 0), r8 < V7X_SUBLANES - d
        else:
            ush, ash, ok = pltpu.roll(u, d, 0), pltpu.roll(a, d, 0), r8 >= d
        u = u + a * jnp.where(ok, ush, 0.0)
        a = a * jnp.where(ok, ash, 1.0)
        d *= 2
    return a, u


def _rglru_dir(xp_ref, hs_ref, cw_ref, cb_ref, w_ref, ba_ref, bx_ref, lam_ref, o_ref, carry_ref, reverse):
    seg = RG_SEG
    nm = CONV_T // seg
    nv = nm // V7X_SUBLANES
    first = V7X_SUBLANES - (D_CONV - 1) // 2
    r8 = lax.broadcasted_iota(jnp.int32, (V7X_SUBLANES, D_BLOCK_DIM), 0)
    tiny = float(jnp.finfo(F32).tiny)
    for blk in range(D_BLOCKS):
        cols = slice(blk * D_BLOCK_DIM, (blk + 1) * D_BLOCK_DIM)
        xcs = []
        for k in range(seg):
            xc = cb_ref[:, cols]
            for j in range(D_CONV):
                xc = xc + xp_ref[blk, pl.ds(first + k + j, nm, stride=seg), :] * cw_ref[j:j + 1, cols]
            xcs.append(xc)
        xc = jnp.concatenate(xcs, axis=0)
        pr = _dot(xc.astype(BF16), w_ref[blk])
        r = jax.nn.sigmoid(pr[:, 0:D_BLOCK_DIM] + ba_ref[:, cols])
        gate_i = jax.nn.sigmoid(pr[:, D_BLOCK_DIM:] + bx_ref[:, cols])
        log_a = (D_C * r) * _log_sigmoid(lam_ref[:, cols])
        a = jnp.exp(log_a)
        th = jnp.tanh(log_a)
        y = -2.0 * th / (1.0 - th)
        u = (y * lax.rsqrt(jnp.maximum(y, tiny))) * (gate_i * xc)
        ak = [a[k * nm:(k + 1) * nm] for k in range(seg)]
        uk = [u[k * nm:(k + 1) * nm] for k in range(seg)]
        for k in (range(seg - 2, -1, -1) if reverse else range(1, seg)):
            kp = k + 1 if reverse else k - 1
            uk[k] = uk[k] + ak[k] * uk[kp]
            ak[k] = ak[k] * ak[kp]
        edge = 0 if reverse else seg - 1
        seg_a, seg_u = _sublane_scan(ak[edge], uk[edge], reverse)
        c = carry_ref[:, cols]
        cins = [None] * nv
        for g in (range(nv - 1, -1, -1) if reverse else range(nv)):
            rows = slice(g * V7X_SUBLANES, (g + 1) * V7X_SUBLANES)
            hend = seg_u[rows] + seg_a[rows] * c
            if reverse:
                cins[g] = jnp.where(r8 == V7X_SUBLANES - 1, c, pltpu.roll(hend, V7X_SUBLANES - 1, 0))
                c = hend[0:1]
            else:
                cins[g] = jnp.where(r8 == 0, c, pltpu.roll(hend, 1, 0))
                c = hend[V7X_SUBLANES - 1:V7X_SUBLANES]
        carry_ref[:, cols] = c
        cin = jnp.concatenate(cins, axis=0)
        for k in range(seg):
            hs_ref[blk, pl.ds(k, nm, stride=seg), :] = uk[k] + ak[k] * cin
        o_ref[:, cols] = hs_ref[blk].astype(BF16)


def _rglru_kernel(pf_ref, mf_ref, nf_ref, pb_ref, mb_ref, nb_ref, cw_ref, cb_ref, wf_ref, wb_ref,
                  ba_ref, bx_ref, lam_ref, hf_ref, hb_ref, xpf_ref, xpb_ref, hsf_ref, hsb_ref, cf_ref, cr_ref):
    i = pl.program_id(1)
    last = pl.num_programs(1) - 1

    @pl.when(i == 0)
    def _():
        cf_ref[...] = jnp.zeros_like(cf_ref)
        cr_ref[...] = jnp.zeros_like(cr_ref)

    _fill_padded_blocks(xpf_ref, pf_ref, mf_ref, nf_ref, i > 0, i < last)
    _fill_padded_blocks(xpb_ref, pb_ref, mb_ref, nb_ref, i < last, i > 0)
    _rglru_dir(xpf_ref, hsf_ref, cw_ref, cb_ref, wf_ref, ba_ref.at[0], bx_ref.at[0], lam_ref.at[0], hf_ref, cf_ref, False)
    _rglru_dir(xpb_ref, hsb_ref, cw_ref, cb_ref, wb_ref, ba_ref.at[1], bx_ref.at[1], lam_ref.at[1], hb_ref, cr_ref, True)


def _rglru(xr, conv_w, conv_b, w_f, w_b, ba, bx, lam, bsz, seq):
    n = bsz * seq
    nt = seq // CONV_T
    o = jax.ShapeDtypeStruct((n, D_WIDTH), BF16)
    vec2 = _resident((2, 1, D_WIDTH))
    carry = pltpu.VMEM((1, D_WIDTH), F32)
    padded = pltpu.VMEM((D_BLOCKS, CONV_T + 2 * V7X_SUBLANES, D_BLOCK_DIM), F32)
    slabs = pltpu.VMEM((D_BLOCKS, CONV_T, D_BLOCK_DIM), F32)
    return pl.pallas_call(
        _rglru_kernel,
        out_shape=[o, o],
        grid=(bsz, nt),
        in_specs=(_halo_specs(D_WIDTH, bsz, seq, CONV_T, V7X_SUBLANES, False)
                  + _halo_specs(D_WIDTH, bsz, seq, CONV_T, V7X_SUBLANES, True)
                  + [_resident((D_CONV, D_WIDTH)), _resident((1, D_WIDTH)), _resident(w_f.shape),
                     _resident(w_b.shape), vec2, vec2, vec2]),
        out_specs=[pl.BlockSpec((CONV_T, D_WIDTH), lambda b, i: (b * nt + i, 0)),
                   pl.BlockSpec((CONV_T, D_WIDTH), lambda b, i: (b * nt + nt - 1 - i, 0))],
        scratch_shapes=[padded, padded, slabs, slabs, carry, carry],
        compiler_params=_params(("parallel", "arbitrary"), 12 * CONV_T * D_WIDTH * 4 + (8 << 20)),
        name="rglru",
    )(xr, xr, xr, xr, xr, xr, conv_w, conv_b, w_f, w_b, ba, bx, lam)


def _odd_out_kernel(of_ref, ob_ref, og_ref, hf_ref, hb_ref, yg_ref, x_ref, w_ref, cg_ref, xo_ref):
    c_parts = []
    for h in range(C_HEADS):
        cols = slice(h * C_DV, (h + 1) * C_DV)
        o = of_ref[:, cols].astype(F32) + ob_ref[:, cols].astype(F32)
        og = og_ref[:, cols].astype(F32)
        c_parts.append((_rms(o, cg_ref[...]) * (og * jax.nn.sigmoid(og))).astype(BF16))
    c_out = jnp.concatenate(c_parts, axis=1)
    hsum = hf_ref[...].astype(F32) + hb_ref[...].astype(F32)
    d_out = (hsum * jax.nn.gelu(yg_ref[...].astype(F32))).astype(BF16)
    _residual_out(x_ref, xo_ref,
                  lambda lo, hi: _dot(c_out, w_ref[0:C_V, lo:hi]) + _dot(d_out, w_ref[C_V:C_V + D_WIDTH, lo:hi]))


def _odd_out(o_f, o_b, og, h_f, h_b, yg, x, w_out, c_norm_g):
    return _out_call(_odd_out_kernel, "odd_out", [o_f, o_b, og, h_f, h_b, yg], x, w_out, [c_norm_g])


def _prepare(p):
    row = lambda v: v.reshape(1, -1).astype(F32)
    stacks = {name: p[name].astype(BF16) for name in
              ("ffn1_w_in", "ffn1_w_out", "ffn2_w_in", "ffn2_w_out", "ev_w_in", "ev_w_out", "od_w_out")}
    layers = []
    for layer in range(DEPTH):
        j = layer // 2
        lw = {
            "ln_ffn1": row(p["ln_ffn1"][layer]), "ln_mix": row(p["ln_mix"][layer]), "ln_ffn2": row(p["ln_ffn2"][layer]),
            "ffn1_w_in": (stacks["ffn1_w_in"], layer), "ffn1_w_out": (stacks["ffn1_w_out"], layer),
            "ffn2_w_in": (stacks["ffn2_w_in"], layer), "ffn2_w_out": (stacks["ffn2_w_out"], layer),
        }
        if layer % 2 == 0:
            lw.update({
                "w_in": (stacks["ev_w_in"], j), "w_out": (stacks["ev_w_out"], j),
                "q_gain": row(p["a_q_gain"][j]), "k_gain": row(p["a_k_gain"][j]), "sink": p["a_sink"][j].astype(F32),
                "conv_w": p["b_conv_w"][j].astype(F32), "conv_b": row(p["b_conv_b"][j]),
                "norm_g": row(p["b_norm_g"][j]), "norm_b": row(p["b_norm_b"][j]),
            })
        else:
            w = p["od_w_in"][j]
            o4 = 2 * C_QK + 2 * C_V
            o5 = o4 + 2 * C_RANK
            w_c = jnp.concatenate([w[:, :o4], w[:, o4:o5], jnp.zeros((D_MODEL, LR_PAD - 2 * C_RANK), w.dtype)], axis=1)
            gup = jnp.zeros((LR_PAD, 2 * C_QK), F32)
            gup = gup.at[0:C_RANK, 0:C_QK].set(p["c_gate_up"][j, 0])
            gup = gup.at[C_RANK:2 * C_RANK, C_QK:2 * C_QK].set(p["c_gate_up"][j, 1])
            lw.update({
                "w_c": w_c.astype(BF16), "w_d": w[:, o5:].astype(BF16), "w_out": (stacks["od_w_out"], j),
                "gup": gup.astype(BF16), "gbias": p["c_gate_bias"][j].reshape(1, 2 * C_QK).astype(F32),
                "c_norm_g": row(p["c_norm_g"][j]),
                "conv_w": p["d_conv_w"][j].astype(F32), "conv_b": row(p["d_conv_b"][j]),
                "w_f": jnp.concatenate([p["d_wa"][j, 0], p["d_wx"][j, 0]], axis=-1).astype(BF16),
                "w_b": jnp.concatenate([p["d_wa"][j, 1], p["d_wx"][j, 1]], axis=-1).astype(BF16),
                "ba": p["d_ba"][j].reshape(2, 1, D_WIDTH).astype(F32), "bx": p["d_bx"][j].reshape(2, 1, D_WIDTH).astype(F32),
                "lam": p["d_lambda"][j].reshape(2, 1, D_WIDTH).astype(F32),
            })
        layers.append(lw)
    return layers


def _run_trunk(x3, layers):
    bsz, seq, d = x3.shape
    x = x3.reshape(bsz * seq, d)
    for layer, lw in enumerate(layers):
        x, xn = _ffn(x, lw["ln_ffn1"], lw["ffn1_w_in"], lw["ffn1_w_out"], lw["ln_mix"])
        if layer % 2 == 0:
            q, k, v, glu = _even_in(xn, lw["w_in"], lw["q_gain"], lw["k_gain"])
            a = _attention(q, k, v, lw["sink"], bsz, seq)
            c = _conv_b(glu, lw["conv_w"], lw["conv_b"], lw["norm_g"], lw["norm_b"], bsz, seq)
            x = _even_out(a, c, x, lw["w_out"])
        else:
            q, k, v, og, la_f, la_b = _odd_in_c(xn, lw["w_c"], lw["gup"], lw["gbias"])
            xr, yg = _odd_in_d(xn, lw["w_d"])
            o_f, o_b = _gla(q, k, v, la_f, la_b, bsz, seq)
            h_f, h_b = _rglru(xr, lw["conv_w"], lw["conv_b"], lw["w_f"], lw["w_b"], lw["ba"], lw["bx"], lw["lam"],
                              bsz, seq)
            x = _odd_out(o_f, o_b, og, h_f, h_b, yg, x, lw["w_out"], lw["c_norm_g"])
        x, _ = _ffn(x, lw["ln_ffn2"], lw["ffn2_w_in"], lw["ffn2_w_out"])
    return x.reshape(bsz, seq, d)


def kernel(x_prompt, x_sample, ln_ffn1, ffn1_w_in, ffn1_w_out, ln_mix, ln_ffn2, ffn2_w_in, ffn2_w_out, ev_w_in, ev_w_out, a_q_gain, a_k_gain, a_sink, b_conv_w, b_conv_b, b_norm_g, b_norm_b, od_w_in, od_w_out, c_gate_up, c_gate_bias, c_norm_g, d_conv_w, d_conv_b, d_wa, d_ba, d_wx, d_bx, d_lambda):
    layers = _prepare(dict(
        ln_ffn1=ln_ffn1, ffn1_w_in=ffn1_w_in, ffn1_w_out=ffn1_w_out, ln_mix=ln_mix, ln_ffn2=ln_ffn2,
        ffn2_w_in=ffn2_w_in, ffn2_w_out=ffn2_w_out, ev_w_in=ev_w_in, ev_w_out=ev_w_out, a_q_gain=a_q_gain,
        a_k_gain=a_k_gain, a_sink=a_sink, b_conv_w=b_conv_w, b_conv_b=b_conv_b, b_norm_g=b_norm_g,
        b_norm_b=b_norm_b, od_w_in=od_w_in, od_w_out=od_w_out, c_gate_up=c_gate_up, c_gate_bias=c_gate_bias,
        c_norm_g=c_norm_g, d_conv_w=d_conv_w, d_conv_b=d_conv_b, d_wa=d_wa, d_ba=d_ba, d_wx=d_wx, d_bx=d_bx,
        d_lambda=d_lambda))
    return (_run_trunk(x_prompt, layers), _run_trunk(x_sample, layers))
```

```python
import functools

import jax
import jax.numpy as jnp
from jax import lax
from jax.experimental import pallas as pl
from jax.experimental.pallas import tpu as pltpu

F32 = jnp.float32
BF16 = jnp.bfloat16

D_MODEL = 4096
DEPTH = 4
D_FF = 3 * D_MODEL // 8
EPS = 1e-6
MIX_HALF = D_MODEL // 4
A_HEAD_DIM = 128
A_HEADS = MIX_HALF // A_HEAD_DIM
A_KV_HEADS = 2
A_GROUP = A_HEADS // A_KV_HEADS
A_WINDOW = 128
A_BLOCK = 128
A_Q = A_HEADS * A_HEAD_DIM
A_KV = A_KV_HEADS * A_HEAD_DIM
B_CH = MIX_HALF
B_CONV = 31
C_HEADS = 4
C_DV = MIX_HALF // C_HEADS
C_DK = C_DV // 2
C_RANK = 16
C_TAU = 16.0
C_QK = C_HEADS * C_DK
C_V = C_HEADS * C_DV
D_WIDTH = MIX_HALF
D_BLOCKS = 8
D_BLOCK_DIM = D_WIDTH // D_BLOCKS
D_CONV = 4
D_C = 8.0

V7X_LANES = 128
V7X_SUBLANES = 8
V7X_VMEM_BYTES = 64 * 1024 * 1024

TM = 256
TM_IN = 512
FFN_PHASES = 1
FF_CHUNK = 512
OUT_CHUNK = 1024
CONV_T = 256
CONV_HALO = 16
CONV_RC = 128
CONV_STRIDE = 2
ATTN_BLOCKS = 2
RG_SEG = 4
GLA_T = 256
GLA_C = 64
GLA_SUB = 16
LR_PAD = 128


def _vmem_limit(nbytes):
    return int(min(nbytes * 5 // 4 + (4 << 20), V7X_VMEM_BYTES - (4 << 20)))


def _params(sem, nbytes):
    return pltpu.CompilerParams(dimension_semantics=sem, vmem_limit_bytes=_vmem_limit(nbytes))


def _resident(shape):
    nd = len(shape)
    return pl.BlockSpec(shape, lambda *_: (0,) * nd, pipeline_mode=pl.Buffered(1))


def _layer_spec(w):
    stack, layer = w
    nd = stack.ndim
    return pl.BlockSpec((None,) + stack.shape[1:], lambda *_: (layer,) + (0,) * (nd - 1),
                        pipeline_mode=pl.Buffered(1))


def _layer_bytes(w):
    stack = w[0]
    return stack.size // stack.shape[0] * stack.dtype.itemsize


def _rows(tm, width):
    return pl.BlockSpec((tm, width), lambda i: (i, 0))


def _dot(a, b):
    return jnp.dot(a, b, preferred_element_type=F32)


def _dot_nt(a, b):
    return lax.dot_general(a, b, (((1,), (1,)), ((), ())), preferred_element_type=F32)


def _dot_tn(a, b):
    return lax.dot_general(a, b, (((0,), (0,)), ((), ())), preferred_element_type=F32)


def _rms(x, g):
    inv = lax.rsqrt(jnp.mean(x * x, axis=-1, keepdims=True) + EPS)
    return (x * inv) * g


def _log_sigmoid(x):
    return jnp.minimum(x, 0.0) - jnp.log1p(jnp.exp(-jnp.abs(x)))


def _scaled_dot(xg, w, inv):
    return _dot(xg, w) * jnp.concatenate([inv] * (w.shape[1] // V7X_LANES), axis=1)


def _norm_specs(d, tm):
    return [_rows(tm, d), _rows(tm, V7X_LANES)]


def _ffn_kernel(x_ref, g_ref, w_in_ref, w_out_ref, *rest, emit_norm):
    if emit_norm:
        gn_ref, xo_ref, xgo_ref, invo_ref, xg_s, act_s, ssq_s = rest
    else:
        xo_ref, xg_s, act_s = rest
    width = D_MODEL // FFN_PHASES
    ph = pl.program_id(1)

    def down_proj(k):
        act = act_s[...]
        ssq = jnp.zeros((x_ref.shape[0], 1), F32)
        for c in range(width // OUT_CHUNK):
            lo = k * width + c * OUT_CHUNK
            out_cols = slice(c * OUT_CHUNK, (c + 1) * OUT_CHUNK)
            xnew = x_ref[:, lo:lo + OUT_CHUNK] + 0.5 * _dot(act, w_out_ref[:, lo:lo + OUT_CHUNK])
            xo_ref[:, out_cols] = xnew
            if emit_norm:
                xgo_ref[:, out_cols] = (xnew * gn_ref[:, lo:lo + OUT_CHUNK]).astype(BF16)
                ssq = ssq + jnp.sum(xnew * xnew, axis=-1, keepdims=True)
        if emit_norm:
            if k > 0:
                ssq = ssq + ssq_s[:, 0:1]
            if k < FFN_PHASES - 1:
                ssq_s[...] = jnp.broadcast_to(ssq, ssq_s.shape)
            else:
                inv = lax.rsqrt(ssq * (1.0 / D_MODEL) + EPS)
                invo_ref[...] = jnp.broadcast_to(inv, invo_ref.shape)

    @pl.when(ph == 0)
    def _():
        x = x_ref[...]
        inv = lax.rsqrt(jnp.mean(x * x, axis=-1, keepdims=True) + EPS)
        xg_s[...] = (x * g_ref[...]).astype(BF16)
        xg = xg_s[...]
        for c in range(D_FF // FF_CHUNK):
            lo = c * FF_CHUNK
            gate = _dot(xg, w_in_ref[:, lo:lo + FF_CHUNK]) * inv
            up = _dot(xg, w_in_ref[:, D_FF + lo:D_FF + lo + FF_CHUNK]) * inv
            act_s[:, lo:lo + FF_CHUNK] = (gate * jax.nn.sigmoid(gate) * up).astype(BF16)
        down_proj(0)

    for k in range(1, FFN_PHASES):
        pl.when(ph == k)(functools.partial(down_proj, k))


def _ffn(x, g, w_in, w_out, g_next=None):
    n, d = x.shape
    width = d // FFN_PHASES
    emit_norm = g_next is not None
    out_shape = [jax.ShapeDtypeStruct((n, d), F32)]
    out_specs = [pl.BlockSpec((TM, width), lambda i, p: (i, p))]
    scratch = [pltpu.VMEM((TM, d), BF16), pltpu.VMEM((TM, D_FF), BF16)]
    operands = [x, g, w_in[0], w_out[0]]
    in_specs = [pl.BlockSpec((TM, d), lambda i, p: (i, 0)), _resident((1, d)), _layer_spec(w_in), _layer_spec(w_out)]
    if emit_norm:
        operands.append(g_next)
        in_specs.append(_resident((1, d)))
        out_shape += [jax.ShapeDtypeStruct((n, d), BF16), jax.ShapeDtypeStruct((n, V7X_LANES), F32)]
        out_specs += [pl.BlockSpec((TM, width), lambda i, p: (i, p)),
                      pl.BlockSpec((TM, V7X_LANES), lambda i, p: (i, 0))]
        scratch.append(pltpu.VMEM((TM, V7X_LANES), F32))
    nbytes = (_layer_bytes(w_in) + _layer_bytes(w_out) + 2 * TM * d * 4 + 2 * TM * width * (4 + 2)
              + TM * (d + D_FF) * 2)
    res = pl.pallas_call(
        functools.partial(_ffn_kernel, emit_norm=emit_norm),
        out_shape=out_shape,
        grid=(n // TM, FFN_PHASES),
        in_specs=in_specs,
        out_specs=out_specs,
        scratch_shapes=scratch,
        compiler_params=_params(("parallel", "arbitrary"), nbytes),
        name="ffn",
    )(*operands)
    return (res[0], (res[1], res[2])) if emit_norm else (res[0], None)


def _residual_out(x_ref, xo_ref, branch_fn):
    for c in range(D_MODEL // OUT_CHUNK):
        lo = c * OUT_CHUNK
        xo_ref[:, lo:lo + OUT_CHUNK] = x_ref[:, lo:lo + OUT_CHUNK] + branch_fn(lo, lo + OUT_CHUNK)


def _out_call(kernel, name, row_inputs, x, w_out, vectors):
    n, d = x.shape
    nbytes = _layer_bytes(w_out)
    nbytes += 2 * sum(TM * a.shape[1] * a.dtype.itemsize for a in row_inputs)
    nbytes += 2 * TM * d * (4 + 4) + 2 * TM * OUT_CHUNK * 4
    return pl.pallas_call(
        kernel,
        out_shape=jax.ShapeDtypeStruct((n, d), F32),
        grid=(n // TM,),
        in_specs=([_rows(TM, a.shape[1]) for a in row_inputs] + [_rows(TM, d)]
                  + [_layer_spec(w_out)] + [_resident(v.shape) for v in vectors]),
        out_specs=_rows(TM, d),
        compiler_params=_params(("parallel",), nbytes),
        name=name,
    )(*row_inputs, x, w_out[0], *vectors)


def _head_norm(p, gain):
    inv = lax.rsqrt(jnp.mean(p * p, axis=-1, keepdims=True) + EPS)
    return ((p * inv) * gain).astype(BF16)


def _even_in_kernel(xg_ref, inv_ref, w_ref, qg_ref, kg_ref, q_ref, k_ref, v_ref, glu_ref):
    xg = xg_ref[...]
    inv = inv_ref[...]
    hd = A_HEAD_DIM
    for c in range(A_Q // FF_CHUNK):
        p = _scaled_dot(xg, w_ref[:, c * FF_CHUNK:(c + 1) * FF_CHUNK], inv)
        for h in range(FF_CHUNK // hd):
            q_ref[:, c * FF_CHUNK + h * hd:c * FF_CHUNK + (h + 1) * hd] = _head_norm(
                p[:, h * hd:(h + 1) * hd], qg_ref[...])
    p = _scaled_dot(xg, w_ref[:, A_Q:A_Q + 2 * A_KV], inv)
    for h in range(A_KV_HEADS):
        k_ref[:, h * hd:(h + 1) * hd] = _head_norm(p[:, h * hd:(h + 1) * hd], kg_ref[...])
    v_ref[...] = p[:, A_KV:].astype(BF16)
    u0 = A_Q + 2 * A_KV
    for c in range(B_CH // FF_CHUNK):
        lo = c * FF_CHUNK
        u = _scaled_dot(xg, w_ref[:, u0 + lo:u0 + lo + FF_CHUNK], inv)
        gate = _scaled_dot(xg, w_ref[:, u0 + B_CH + lo:u0 + B_CH + lo + FF_CHUNK], inv)
        glu_ref[:, lo:lo + FF_CHUNK] = u * jax.nn.sigmoid(gate)


def _even_in(xn, w_in, q_gain, k_gain):
    n, d = xn[0].shape
    return pl.pallas_call(
        _even_in_kernel,
        out_shape=[jax.ShapeDtypeStruct((n, A_Q), BF16), jax.ShapeDtypeStruct((n, A_KV), BF16),
                   jax.ShapeDtypeStruct((n, A_KV), BF16), jax.ShapeDtypeStruct((n, B_CH), F32)],
        grid=(n // TM_IN,),
        in_specs=_norm_specs(d, TM_IN) + [_layer_spec(w_in), _resident((1, A_HEAD_DIM)),
                                          _resident((1, A_HEAD_DIM))],
        out_specs=[_rows(TM_IN, A_Q), _rows(TM_IN, A_KV), _rows(TM_IN, A_KV), _rows(TM_IN, B_CH)],
        compiler_params=_params(
            ("parallel",),
            _layer_bytes(w_in) + 2 * TM_IN * (d * 2 + A_Q * 2 + A_KV * 4 + B_CH * 4) + 4 * TM_IN * FF_CHUNK * 4),
        name="even_in",
    )(*xn, w_in[0], q_gain, k_gain)


def _attn_kernel(sink_ref, q_ref, k_ref, v_ref, o_ref, *, seq):
    blk, hd = A_BLOCK, A_HEAD_DIM
    span = 3 * blk
    scale = hd ** -0.5
    for sb in range(ATTN_BLOCKS):
        rows = slice(sb * blk, (sb + 1) * blk)
        n = pl.program_id(1) * ATTN_BLOCKS + sb
        start = pl.multiple_of(jnp.clip((n - 1) * blk, 0, seq - span), blk)
        kb = k_ref[0, pl.ds(start, span), :]
        vb = v_ref[0, pl.ds(start, span), :]
        qpos = n * blk + lax.broadcasted_iota(jnp.int32, (blk, span), 0)
        kpos = start + lax.broadcasted_iota(jnp.int32, (blk, span), 1)
        absrel = jnp.abs(kpos - qpos).astype(F32)
        valid = absrel <= float(A_WINDOW)
        for g in range(A_KV_HEADS):
            kg = kb[:, g * hd:(g + 1) * hd]
            vg = vb[:, g * hd:(g + 1) * hd]
            qg = jnp.concatenate(
                [q_ref[rows, (g * A_GROUP + j) * hd:(g * A_GROUP + j + 1) * hd] for j in range(A_GROUP)], axis=0)
            s = _dot_nt(qg, kg) * scale
            probs = []
            for j in range(A_GROUP):
                h = g * A_GROUP + j
                slope = 2.0 ** (-8.0 * (h + 1) / A_HEADS)
                sj = jnp.where(valid, s[j * blk:(j + 1) * blk] - slope * absrel, -jnp.inf)
                sink = sink_ref[h]
                m = jnp.maximum(jnp.max(sj, axis=-1, keepdims=True), sink)
                p = jnp.exp(sj - m)
                den = jnp.sum(p, axis=-1, keepdims=True) + jnp.exp(sink - m)
                probs.append((p * (1.0 / den)).astype(BF16))
            og = _dot(jnp.concatenate(probs, axis=0), vg)
            for j in range(A_GROUP):
                h = g * A_GROUP + j
                o_ref[rows, h * hd:(h + 1) * hd] = og[j * blk:(j + 1) * blk].astype(BF16)


def _attention(q, k, v, sink, bsz, seq):
    n = bsz * seq
    rows = ATTN_BLOCKS * A_BLOCK
    nb = seq // rows
    k3 = k.reshape(bsz, seq, A_KV)
    v3 = v.reshape(bsz, seq, A_KV)
    return pl.pallas_call(
        functools.partial(_attn_kernel, seq=seq),
        out_shape=jax.ShapeDtypeStruct((n, A_Q), BF16),
        grid=(bsz, nb),
        in_specs=[pl.BlockSpec(memory_space=pltpu.SMEM),
                  pl.BlockSpec((rows, A_Q), lambda b, i: (b * nb + i, 0)),
                  pl.BlockSpec((1, seq, A_KV), lambda b, i: (b, 0, 0)),
                  pl.BlockSpec((1, seq, A_KV), lambda b, i: (b, 0, 0))],
        out_specs=pl.BlockSpec((rows, A_Q), lambda b, i: (b * nb + i, 0)),
        compiler_params=_params(("parallel", "arbitrary"),
                                4 * seq * A_KV * 2 + 4 * rows * A_Q * 2 + 16 * A_BLOCK * 3 * A_BLOCK * 4),
        name="window_attn",
    )(sink, q, k3, v3)


def _halo_specs(width, bsz, seq, tile, halo, reverse):
    nt = seq // tile
    per_tile = tile // halo
    per_seq = seq // halo

    def t_of(i):
        return (nt - 1 - i) if reverse else i

    def prev_map(b, i):
        t = t_of(i)
        return (b * per_seq + jnp.maximum(t * per_tile - 1, 0), 0)

    def main_map(b, i):
        return (b * nt + t_of(i), 0)

    def next_map(b, i):
        t = t_of(i)
        return (b * per_seq + jnp.minimum((t + 1) * per_tile, per_seq - 1), 0)

    return [pl.BlockSpec((halo, width), prev_map), pl.BlockSpec((tile, width), main_map),
            pl.BlockSpec((halo, width), next_map)]


def _fill_padded_blocks(xp_ref, prev_ref, main_ref, next_ref, has_prev, has_next):
    halo = prev_ref.shape[0]
    tile = main_ref.shape[0]
    for s in range(xp_ref.shape[0]):
        cols = slice(s * V7X_LANES, (s + 1) * V7X_LANES)
        xp_ref[s, 0:halo, :] = jnp.where(has_prev, prev_ref[:, cols], 0.0)
        xp_ref[s, halo:halo + tile, :] = main_ref[:, cols]
        xp_ref[s, halo + tile:halo + tile + halo, :] = jnp.where(has_next, next_ref[:, cols], 0.0)


def _convb_kernel(prev_ref, main_ref, next_ref, w_ref, b_ref, g_ref, beta_ref, o_ref, xp_ref, y_ref):
    i = pl.program_id(1)
    _fill_padded_blocks(xp_ref, prev_ref, main_ref, next_ref, i > 0, i < pl.num_programs(1) - 1)
    first = CONV_HALO - (B_CONV - 1) // 2
    nrows = CONV_RC // CONV_STRIDE
    groups = nrows // V7X_SUBLANES
    phases = [(r0, p) for r0 in range(0, CONV_T, CONV_RC) for p in range(CONV_STRIDE)]
    for s in range(B_CH // V7X_LANES):
        cols = slice(s * V7X_LANES, (s + 1) * V7X_LANES)
        accs = [None] * len(phases)
        for j in range(B_CONV):
            wj = jnp.broadcast_to(w_ref[j:j + 1, cols], (V7X_SUBLANES, V7X_LANES))[None]
            for idx, (r0, p) in enumerate(phases):
                x = xp_ref[s, pl.ds(r0 + p + first + j, nrows, stride=CONV_STRIDE), :]
                term = x.reshape(groups, V7X_SUBLANES, V7X_LANES) * wj
                accs[idx] = term if accs[idx] is None else accs[idx] + term
        for idx, (r0, p) in enumerate(phases):
            y_ref[s, pl.ds(r0 + p, nrows, stride=CONV_STRIDE), :] = (
                accs[idx].reshape(nrows, V7X_LANES) + b_ref[:, cols])
    y = jnp.concatenate([y_ref[s] for s in range(B_CH // V7X_LANES)], axis=1)
    mu = jnp.mean(y, axis=-1, keepdims=True)
    yc = y - mu
    var = jnp.mean(yc * yc, axis=-1, keepdims=True)
    yn = (yc * lax.rsqrt(var + EPS)) * g_ref[...] + beta_ref[...]
    o_ref[...] = (yn * jax.nn.sigmoid(yn)).astype(BF16)


def _conv_b(glu, conv_w, conv_b, norm_g, norm_b, bsz, seq):
    n = bsz * seq
    nt = seq // CONV_T
    vec = _resident((1, B_CH))
    return pl.pallas_call(
        _convb_kernel,
        out_shape=jax.ShapeDtypeStruct((n, B_CH), BF16),
        grid=(bsz, nt),
        in_specs=_halo_specs(B_CH, bsz, seq, CONV_T, CONV_HALO, False) + [_resident((B_CONV, B_CH)), vec, vec, vec],
        out_specs=pl.BlockSpec((CONV_T, B_CH), lambda b, i: (b * nt + i, 0)),
        scratch_shapes=[pltpu.VMEM((B_CH // V7X_LANES, CONV_T + 2 * CONV_HALO, V7X_LANES), F32),
                        pltpu.VMEM((B_CH // V7X_LANES, CONV_T, V7X_LANES), F32)],
        compiler_params=_params(("parallel", "arbitrary"), 8 * CONV_T * B_CH * 4),
        name="conformer_conv",
    )(glu, glu, glu, conv_w, conv_b, norm_g, norm_b)


def _even_out_kernel(a_ref, c_ref, x_ref, w_ref, xo_ref):
    a = a_ref[...]
    c = c_ref[...]
    _residual_out(x_ref, xo_ref,
                  lambda lo, hi: _dot(a, w_ref[0:A_Q, lo:hi]) + _dot(c, w_ref[A_Q:A_Q + B_CH, lo:hi]))


def _even_out(a, c, x, w_out):
    return _out_call(_even_out_kernel, "even_out", [a, c], x, w_out, [])


def _odd_in_c_kernel(xg_ref, inv_ref, w_ref, gup_ref, gb_ref, q_ref, k_ref, v_ref, og_ref, laf_ref, lab_ref):
    xg = xg_ref[...]
    inv = inv_ref[...]
    o = 2 * C_QK + 2 * C_V
    lr = _scaled_dot(xg, w_ref[:, o:o + LR_PAD], inv).astype(BF16)
    pre = _dot(lr, gup_ref[...]) + gb_ref[...]
    loga = _log_sigmoid(pre) * (1.0 / C_TAU)
    laf_ref[...] = loga[:, 0:C_QK]
    lab_ref[...] = loga[:, C_QK:2 * C_QK]
    q_ref[...] = _scaled_dot(xg, w_ref[:, 0:C_QK], inv) * (C_DK ** -0.5)
    k_ref[...] = _scaled_dot(xg, w_ref[:, C_QK:2 * C_QK], inv)
    o = 2 * C_QK
    for c in range(C_V // FF_CHUNK):
        lo = c * FF_CHUNK
        v_ref[:, lo:lo + FF_CHUNK] = _scaled_dot(xg, w_ref[:, o + lo:o + lo + FF_CHUNK], inv).astype(BF16)
        og_ref[:, lo:lo + FF_CHUNK] = _scaled_dot(
            xg, w_ref[:, o + C_V + lo:o + C_V + lo + FF_CHUNK], inv).astype(BF16)


def _odd_in_c(xn, w_c, gup, gbias):
    n, d = xn[0].shape
    qk = jax.ShapeDtypeStruct((n, C_QK), F32)
    vv = jax.ShapeDtypeStruct((n, C_V), BF16)
    return pl.pallas_call(
        _odd_in_c_kernel,
        out_shape=[qk, qk, vv, vv, qk, qk],
        grid=(n // TM_IN,),
        in_specs=_norm_specs(d, TM_IN) + [_resident(w_c.shape), _resident(gup.shape), _resident(gbias.shape)],
        out_specs=[_rows(TM_IN, C_QK), _rows(TM_IN, C_QK), _rows(TM_IN, C_V), _rows(TM_IN, C_V),
                   _rows(TM_IN, C_QK), _rows(TM_IN, C_QK)],
        compiler_params=_params(
            ("parallel",),
            w_c.size * 2 + 2 * TM_IN * (d * 2 + 4 * C_QK * 4 + 2 * C_V * 2) + 6 * TM_IN * FF_CHUNK * 4),
        name="odd_in_c",
    )(*xn, w_c, gup, gbias)


def _odd_in_d_kernel(xg_ref, inv_ref, w_ref, xr_ref, yg_ref):
    xg = xg_ref[...]
    inv = inv_ref[...]
    for c in range(D_WIDTH // FF_CHUNK):
        lo = c * FF_CHUNK
        xr_ref[:, lo:lo + FF_CHUNK] = _scaled_dot(xg, w_ref[:, lo:lo + FF_CHUNK], inv)
        yg_ref[:, lo:lo + FF_CHUNK] = _scaled_dot(
            xg, w_ref[:, D_WIDTH + lo:D_WIDTH + lo + FF_CHUNK], inv).astype(BF16)


def _odd_in_d(xn, w_d):
    n, d = xn[0].shape
    return pl.pallas_call(
        _odd_in_d_kernel,
        out_shape=[jax.ShapeDtypeStruct((n, D_WIDTH), F32), jax.ShapeDtypeStruct((n, D_WIDTH), BF16)],
        grid=(n // TM_IN,),
        in_specs=_norm_specs(d, TM_IN) + [_resident(w_d.shape)],
        out_specs=[_rows(TM_IN, D_WIDTH), _rows(TM_IN, D_WIDTH)],
        compiler_params=_params(("parallel",),
                                w_d.size * 2 + 2 * TM_IN * (d * 2 + D_WIDTH * 6) + 4 * TM_IN * FF_CHUNK * 4),
        name="odd_in_d",
    )(*xn, w_d)


def _split3(x):
    hi = x.astype(BF16)
    r1 = x - hi.astype(F32)
    mid = r1.astype(BF16)
    lo = (r1 - mid.astype(F32)).astype(BF16)
    return hi, mid, lo


def _gla_prep(q, k, b, reverse):
    cc, sub = GLA_C, GLA_SUB
    nsub = cc // sub
    r128 = lax.broadcasted_iota(jnp.int32, (cc, C_DK), 0)
    zero_row = jnp.zeros((1, C_DK), F32)
    if reverse:
        refs = [b[(i + 1) * sub:(i + 1) * sub + 1] for i in range(nsub - 1)] + [zero_row]
        b_end = b[0:1]
    else:
        refs = [zero_row] + [b[i * sub - 1:i * sub] for i in range(1, nsub)]
        b_end = b[cc - 1:cc]
    ref_rows = jnp.concatenate([jnp.broadcast_to(r, (sub, C_DK)) for r in refs], axis=0)
    qt = q * jnp.exp(b - ref_rows)
    q_inter = (qt * jnp.exp(ref_rows)).astype(BF16)
    kvars = []
    for i in range(nsub):
        seen = (r128 >= i * sub) if reverse else (r128 < (i + 1) * sub)
        kvars.append((k * jnp.exp(jnp.where(seen, refs[i] - b, -1e30))).astype(BF16))
    k_out = (k * jnp.exp(b_end - b)).astype(BF16)
    return qt.astype(BF16), q_inter, jnp.concatenate(kvars, axis=0), k_out, jnp.exp(b_end)


def _gla_kernel(qf_ref, kf_ref, vf_ref, laf_ref, qb_ref, kb_ref, vb_ref, lab_ref, of_ref, ob_ref, sf_ref, sb_ref):
    @pl.when(pl.program_id(1) == 0)
    def _():
        sf_ref[...] = jnp.zeros_like(sf_ref)
        sb_ref[...] = jnp.zeros_like(sb_ref)

    cc, sub, tt = GLA_C, GLA_SUB, GLA_T
    nch = tt // cc
    dirs = ((qf_ref, kf_ref, vf_ref, laf_ref, of_ref, sf_ref, False),
            (qb_ref, kb_ref, vb_ref, lab_ref, ob_ref, sb_ref, True))
    units = [(d, c, h) for d in range(2) for c in range(nch) for h in range(C_HEADS)]

    row = lax.broadcasted_iota(jnp.int32, (tt, tt), 0)
    col = lax.broadcasted_iota(jnp.int32, (tt, tt), 1)
    same_chunk = (row // cc) == (col // cc)
    b_tile = []
    for (_, _, _, la_ref, _, _, reverse) in dirs:
        tri = (same_chunk & ((col >= row) if reverse else (col <= row))).astype(BF16)
        hi, mid, lo = _split3(la_ref[...])
        b_tile.append(_dot(tri, hi) + _dot(tri, mid) + _dot(tri, lo))

    prep = {}
    for (d, c, h) in units:
        q_ref, k_ref, _, _, _, _, reverse = dirs[d]
        rows = slice(c * cc, (c + 1) * cc)
        cols = slice(h * C_DK, (h + 1) * C_DK)
        prep[d, c, h] = _gla_prep(q_ref[rows, cols], k_ref[rows, cols], b_tile[d][rows, cols], reverse)

    scores = {u: _dot_nt(prep[u][0], prep[u][2]) for u in units}

    trow = lax.broadcasted_iota(jnp.int32, (cc, V7X_LANES), 0)
    lane = lax.broadcasted_iota(jnp.int32, (cc, V7X_LANES), 1)
    own_half = (lane // cc) == ((trow // sub) % 2)
    first_pair = (trow // sub) < 2
    att = {}
    for u in units:
        reverse = dirs[u[0]][6]
        key = lane % cc
        keep = own_half & ((key >= trow) if reverse else (key <= trow))
        r = scores[u]
        att[u] = jnp.where(keep, jnp.where(first_pair, r[:, 0:V7X_LANES], r[:, V7X_LANES:2 * V7X_LANES]), 0.0).astype(BF16)

    o_intra, kv = {}, {}
    for (d, c, h) in units:
        v = dirs[d][2][c * cc:(c + 1) * cc, h * C_DV:(h + 1) * C_DV]
        o_intra[d, c, h] = _dot(att[d, c, h], jnp.concatenate([v, v], axis=0))
        kv[d, c, h] = _dot_tn(v, prep[d, c, h][3])

    state = {(d, h): dirs[d][5][h] for d in range(2) for h in range(C_HEADS)}
    for step in range(nch):
        for d in range(2):
            c = (nch - 1 - step) if dirs[d][6] else step
            outs = []
            for h in range(C_HEADS):
                st = state[d, h]
                outs.append(o_intra[d, c, h] + _dot_nt(prep[d, c, h][1], st.astype(BF16)))
                state[d, h] = prep[d, c, h][4] * st + kv[d, c, h]
            dirs[d][4][c * cc:(c + 1) * cc, :] = jnp.concatenate(outs, axis=1).astype(BF16)
    for d in range(2):
        for h in range(C_HEADS):
            dirs[d][5][h] = state[d, h]


def _gla(q, k, v, la_f, la_b, bsz, seq):
    n = bsz * seq
    nt = seq // GLA_T

    def fwd(width):
        return pl.BlockSpec((GLA_T, width), lambda b, i: (b * nt + i, 0))

    def bwd(width):
        return pl.BlockSpec((GLA_T, width), lambda b, i: (b * nt + nt - 1 - i, 0))

    o = jax.ShapeDtypeStruct((n, C_V), BF16)
    st = pltpu.VMEM((C_HEADS, C_DV, C_DK), F32)
    return pl.pallas_call(
        _gla_kernel,
        out_shape=[o, o],
        grid=(bsz, nt),
        in_specs=[fwd(C_QK), fwd(C_QK), fwd(C_V), fwd(C_QK), bwd(C_QK), bwd(C_QK), bwd(C_V), bwd(C_QK)],
        out_specs=[fwd(C_V), bwd(C_V)],
        scratch_shapes=[st, st],
        compiler_params=_params(("parallel", "arbitrary"), 4 * GLA_T * (3 * C_QK * 4 + 2 * C_V * 2) + (8 << 20)),
        name="gla",
    )(q, k, v, la_f, q, k, v, la_b)


def _sublane_scan(a, u, reverse):
    t = a.shape[0]
    r8 = lax.broadcasted_iota(jnp.int32, a.shape, 0) & (V7X_SUBLANES - 1)
    d = 1
    while d < V7X_SUBLANES:
        if reverse:
            ush, ash, ok = pltpu.roll(u, t - d, 0), pltpu.roll(a, t - d, 0), r8 < V7X_SUBLANES - d
        else:
            ush, ash, ok = pltpu.roll(u, d, 0), pltpu.roll(a, d, 0), r8 >= d
        u = u + a * jnp.where(ok, ush, 0.0)
        a = a * jnp.where(ok, ash, 1.0)
        d *= 2
    return a, u


def _rglru_dir(xp_ref, hs_ref, cw_ref, cb_ref, w_ref, ba_ref, bx_ref, lam_ref, o_ref, carry_ref, reverse):
    seg = RG_SEG
    nm = CONV_T // seg
    nv = nm // V7X_SUBLANES
    first = V7X_SUBLANES - (D_CONV - 1) // 2
    r8 = lax.broadcasted_iota(jnp.int32, (V7X_SUBLANES, D_BLOCK_DIM), 0)
    tiny = float(jnp.finfo(F32).tiny)
    for blk in range(D_BLOCKS):
        cols = slice(blk * D_BLOCK_DIM, (blk + 1) * D_BLOCK_DIM)
        xcs = []
        for k in range(seg):
            xc = cb_ref[:, cols]
            for j in range(D_CONV):
                xc = xc + xp_ref[blk, pl.ds(first + k + j, nm, stride=seg), :] * cw_ref[j:j + 1, cols]
            xcs.append(xc)
        xc = jnp.concatenate(xcs, axis=0)
        pr = _dot(xc.astype(BF16), w_ref[blk])
        r = jax.nn.sigmoid(pr[:, 0:D_BLOCK_DIM] + ba_ref[:, cols])
        gate_i = jax.nn.sigmoid(pr[:, D_BLOCK_DIM:] + bx_ref[:, cols])
        log_a = (D_C * r) * _log_sigmoid(lam_ref[:, cols])
        a = jnp.exp(log_a)
        th = jnp.tanh(log_a)
        y = -2.0 * th / (1.0 - th)
        u = (y * lax.rsqrt(jnp.maximum(y, tiny))) * (gate_i * xc)
        ak = [a[k * nm:(k + 1) * nm] for k in range(seg)]
        uk = [u[k * nm:(k + 1) * nm] for k in range(seg)]
        for k in (range(seg - 2, -1, -1) if reverse else range(1, seg)):
            kp = k + 1 if reverse else k - 1
            uk[k] = uk[k] + ak[k] * uk[kp]
            ak[k] = ak[k] * ak[kp]
        edge = 0 if reverse else seg - 1
        seg_a, seg_u = _sublane_scan(ak[edge], uk[edge], reverse)
        c = carry_ref[:, cols]
        cins = [None] * nv
        for g in (range(nv - 1, -1, -1) if reverse else range(nv)):
            rows = slice(g * V7X_SUBLANES, (g + 1) * V7X_SUBLANES)
            hend = seg_u[rows] + seg_a[rows] * c
            if reverse:
                cins[g] = jnp.where(r8 == V7X_SUBLANES - 1, c, pltpu.roll(hend, V7X_SUBLANES - 1, 0))
                c = hend[0:1]
            else:
                cins[g] = jnp.where(r8 == 0, c, pltpu.roll(hend, 1, 0))
                c = hend[V7X_SUBLANES - 1:V7X_SUBLANES]
        carry_ref[:, cols] = c
        cin = jnp.concatenate(cins, axis=0)
        for k in range(seg):
            hs_ref[blk, pl.ds(k, nm, stride=seg), :] = uk[k] + ak[k] * cin
        o_ref[:, cols] = hs_ref[blk].astype(BF16)


def _rglru_kernel(pf_ref, mf_ref, nf_ref, pb_ref, mb_ref, nb_ref, cw_ref, cb_ref, wf_ref, wb_ref,
                  ba_ref, bx_ref, lam_ref, hf_ref, hb_ref, xpf_ref, xpb_ref, hsf_ref, hsb_ref, cf_ref, cr_ref):
    i = pl.program_id(1)
    last = pl.num_programs(1) - 1

    @pl.when(i == 0)
    def _():
        cf_ref[...] = jnp.zeros_like(cf_ref)
        cr_ref[...] = jnp.zeros_like(cr_ref)

    _fill_padded_blocks(xpf_ref, pf_ref, mf_ref, nf_ref, i > 0, i < last)
    _fill_padded_blocks(xpb_ref, pb_ref, mb_ref, nb_ref, i < last, i > 0)
    _rglru_dir(xpf_ref, hsf_ref, cw_ref, cb_ref, wf_ref, ba_ref.at[0], bx_ref.at[0], lam_ref.at[0], hf_ref, cf_ref, False)
    _rglru_dir(xpb_ref, hsb_ref, cw_ref, cb_ref, wb_ref, ba_ref.at[1], bx_ref.at[1], lam_ref.at[1], hb_ref, cr_ref, True)


def _rglru(xr, conv_w, conv_b, w_f, w_b, ba, bx, lam, bsz, seq):
    n = bsz * seq
    nt = seq // CONV_T
    o = jax.ShapeDtypeStruct((n, D_WIDTH), BF16)
    vec2 = _resident((2, 1, D_WIDTH))
    carry = pltpu.VMEM((1, D_WIDTH), F32)
    padded = pltpu.VMEM((D_BLOCKS, CONV_T + 2 * V7X_SUBLANES, D_BLOCK_DIM), F32)
    slabs = pltpu.VMEM((D_BLOCKS, CONV_T, D_BLOCK_DIM), F32)
    return pl.pallas_call(
        _rglru_kernel,
        out_shape=[o, o],
        grid=(bsz, nt),
        in_specs=(_halo_specs(D_WIDTH, bsz, seq, CONV_T, V7X_SUBLANES, False)
                  + _halo_specs(D_WIDTH, bsz, seq, CONV_T, V7X_SUBLANES, True)
                  + [_resident((D_CONV, D_WIDTH)), _resident((1, D_WIDTH)), _resident(w_f.shape),
                     _resident(w_b.shape), vec2, vec2, vec2]),
        out_specs=[pl.BlockSpec((CONV_T, D_WIDTH), lambda b, i: (b * nt + i, 0)),
                   pl.BlockSpec((CONV_T, D_WIDTH), lambda b, i: (b * nt + nt - 1 - i, 0))],
        scratch_shapes=[padded, padded, slabs, slabs, carry, carry],
        compiler_params=_params(("parallel", "arbitrary"), 12 * CONV_T * D_WIDTH * 4 + (8 << 20)),
        name="rglru",
    )(xr, xr, xr, xr, xr, xr, conv_w, conv_b, w_f, w_b, ba, bx, lam)


def _odd_out_kernel(of_ref, ob_ref, og_ref, hf_ref, hb_ref, yg_ref, x_ref, w_ref, cg_ref, xo_ref):
    c_parts = []
    for h in range(C_HEADS):
        cols = slice(h * C_DV, (h + 1) * C_DV)
        o = of_ref[:, cols].astype(F32) + ob_ref[:, cols].astype(F32)
        og = og_ref[:, cols].astype(F32)
        c_parts.append((_rms(o, cg_ref[...]) * (og * jax.nn.sigmoid(og))).astype(BF16))
    c_out = jnp.concatenate(c_parts, axis=1)
    hsum = hf_ref[...].astype(F32) + hb_ref[...].astype(F32)
    d_out = (hsum * jax.nn.gelu(yg_ref[...].astype(F32))).astype(BF16)
    _residual_out(x_ref, xo_ref,
                  lambda lo, hi: _dot(c_out, w_ref[0:C_V, lo:hi]) + _dot(d_out, w_ref[C_V:C_V + D_WIDTH, lo:hi]))


def _odd_out(o_f, o_b, og, h_f, h_b, yg, x, w_out, c_norm_g):
    return _out_call(_odd_out_kernel, "odd_out", [o_f, o_b, og, h_f, h_b, yg], x, w_out, [c_norm_g])


def _prepare(p):
    row = lambda v: v.reshape(1, -1).astype(F32)
    stacks = {name: p[name].astype(BF16) for name in
              ("ffn1_w_in", "ffn1_w_out", "ffn2_w_in", "ffn2_w_out", "ev_w_in", "ev_w_out", "od_w_out")}
    layers = []
    for layer in range(DEPTH):
        j = layer // 2
        lw = {
            "ln_ffn1": row(p["ln_ffn1"][layer]), "ln_mix": row(p["ln_mix"][layer]), "ln_ffn2": row(p["ln_ffn2"][layer]),
            "ffn1_w_in": (stacks["ffn1_w_in"], layer), "ffn1_w_out": (stacks["ffn1_w_out"], layer),
            "ffn2_w_in": (stacks["ffn2_w_in"], layer), "ffn2_w_out": (stacks["ffn2_w_out"], layer),
        }
        if layer % 2 == 0:
            lw.update({
                "w_in": (stacks["ev_w_in"], j), "w_out": (stacks["ev_w_out"], j),
                "q_gain": row(p["a_q_gain"][j]), "k_gain": row(p["a_k_gain"][j]), "sink": p["a_sink"][j].astype(F32),
                "conv_w": p["b_conv_w"][j].astype(F32), "conv_b": row(p["b_conv_b"][j]),
                "norm_g": row(p["b_norm_g"][j]), "norm_b": row(p["b_norm_b"][j]),
            })
        else:
            w = p["od_w_in"][j]
            o4 = 2 * C_QK + 2 * C_V
            o5 = o4 + 2 * C_RANK
            w_c = jnp.concatenate([w[:, :o4], w[:, o4:o5], jnp.zeros((D_MODEL, LR_PAD - 2 * C_RANK), w.dtype)], axis=1)
            gup = jnp.zeros((LR_PAD, 2 * C_QK), F32)
            gup = gup.at[0:C_RANK, 0:C_QK].set(p["c_gate_up"][j, 0])
            gup = gup.at[C_RANK:2 * C_RANK, C_QK:2 * C_QK].set(p["c_gate_up"][j, 1])
            lw.update({
                "w_c": w_c.astype(BF16), "w_d": w[:, o5:].astype(BF16), "w_out": (stacks["od_w_out"], j),
                "gup": gup.astype(BF16), "gbias": p["c_gate_bias"][j].reshape(1, 2 * C_QK).astype(F32),
                "c_norm_g": row(p["c_norm_g"][j]),
                "conv_w": p["d_conv_w"][j].astype(F32), "conv_b": row(p["d_conv_b"][j]),
                "w_f": jnp.concatenate([p["d_wa"][j, 0], p["d_wx"][j, 0]], axis=-1).astype(BF16),
                "w_b": jnp.concatenate([p["d_wa"][j, 1], p["d_wx"][j, 1]], axis=-1).astype(BF16),
                "ba": p["d_ba"][j].reshape(2, 1, D_WIDTH).astype(F32), "bx": p["d_bx"][j].reshape(2, 1, D_WIDTH).astype(F32),
                "lam": p["d_lambda"][j].reshape(2, 1, D_WIDTH).astype(F32),
            })
        layers.append(lw)
    return layers


def _run_trunk(x3, layers):
    bsz, seq, d = x3.shape
    x = x3.reshape(bsz * seq, d)
    for layer, lw in enumerate(layers):
        x, xn = _ffn(x, lw["ln_ffn1"], lw["ffn1_w_in"], lw["ffn1_w_out"], lw["ln_mix"])
        if layer % 2 == 0:
            q, k, v, glu = _even_in(xn, lw["w_in"], lw["q_gain"], lw["k_gain"])
            a = _attention(q, k, v, lw["sink"], bsz, seq)
            c = _conv_b(glu, lw["conv_w"], lw["conv_b"], lw["norm_g"], lw["norm_b"], bsz, seq)
            x = _even_out(a, c, x, lw["w_out"])
        else:
            q, k, v, og, la_f, la_b = _odd_in_c(xn, lw["w_c"], lw["gup"], lw["gbias"])
            xr, yg = _odd_in_d(xn, lw["w_d"])
            o_f, o_b = _gla(q, k, v, la_f, la_b, bsz, seq)
            h_f, h_b = _rglru(xr, lw["conv_w"], lw["conv_b"], lw["w_f"], lw["w_b"], lw["ba"], lw["bx"], lw["lam"],
                              bsz, seq)
            x = _odd_out(o_f, o_b, og, h_f, h_b, yg, x, lw["w_out"], lw["c_norm_g"])
        x, _ = _ffn(x, lw["ln_ffn2"], lw["ffn2_w_in"], lw["ffn2_w_out"])
    return x.reshape(bsz, seq, d)


def kernel(x_prompt, x_sample, ln_ffn1, ffn1_w_in, ffn1_w_out, ln_mix, ln_ffn2, ffn2_w_in, ffn2_w_out, ev_w_in, ev_w_out, a_q_gain, a_k_gain, a_sink, b_conv_w, b_conv_b, b_norm_g, b_norm_b, od_w_in, od_w_out, c_gate_up, c_gate_bias, c_norm_g, d_conv_w, d_conv_b, d_wa, d_ba, d_wx, d_bx, d_lambda):
    layers = _prepare(dict(
        ln_ffn1=ln_ffn1, ffn1_w_in=ffn1_w_in, ffn1_w_out=ffn1_w_out, ln_mix=ln_mix, ln_ffn2=ln_ffn2,
        ffn2_w_in=ffn2_w_in, ffn2_w_out=ffn2_w_out, ev_w_in=ev_w_in, ev_w_out=ev_w_out, a_q_gain=a_q_gain,
        a_k_gain=a_k_gain, a_sink=a_sink, b_conv_w=b_conv_w, b_conv_b=b_conv_b, b_norm_g=b_norm_g,
        b_norm_b=b_norm_b, od_w_in=od_w_in, od_w_out=od_w_out, c_gate_up=c_gate_up, c_gate_bias=c_gate_bias,
        c_norm_g=c_norm_g, d_conv_w=d_conv_w, d_conv_b=d_conv_b, d_wa=d_wa, d_ba=d_ba, d_wx=d_wx, d_bx=d_bx,
        d_lambda=d_lambda))
    return (_run_trunk(x_prompt, layers), _run_trunk(x_sample, layers))
```

```python
import functools

import jax
import jax.numpy as jnp
from jax import lax
from jax.experimental import pallas as pl
from jax.experimental.pallas import tpu as pltpu

F32 = jnp.float32
BF16 = jnp.bfloat16

D_MODEL = 4096
DEPTH = 4
D_FF = 3 * D_MODEL // 8
EPS = 1e-6
MIX_HALF = D_MODEL // 4
A_HEAD_DIM = 128
A_HEADS = MIX_HALF // A_HEAD_DIM
A_KV_HEADS = 2
A_GROUP = A_HEADS // A_KV_HEADS
A_WINDOW = 128
A_BLOCK = 128
A_Q = A_HEADS * A_HEAD_DIM
A_KV = A_KV_HEADS * A_HEAD_DIM
B_CH = MIX_HALF
B_CONV = 31
C_HEADS = 4
C_DV = MIX_HALF // C_HEADS
C_DK = C_DV // 2
C_RANK = 16
C_TAU = 16.0
C_QK = C_HEADS * C_DK
C_V = C_HEADS * C_DV
D_WIDTH = MIX_HALF
D_BLOCKS = 8
D_BLOCK_DIM = D_WIDTH // D_BLOCKS
D_CONV = 4
D_C = 8.0

V7X_LANES = 128
V7X_SUBLANES = 8
V7X_VMEM_BYTES = 64 * 1024 * 1024

TM = 256
TM_IN = 512
FFN_PHASES = 1
FF_CHUNK = 512
OUT_CHUNK = 1024
MIX_OUT_CHUNK = 2048
CONV_T = 256
CONV_HALO = 16
CONV_RC = 128
CONV_STRIDE = 2
ATTN_BLOCKS = 2
RG_SEG = 4
GLA_T = 256
GLA_C = 64
GLA_SUB = 16
LR_PAD = 128


def _vmem_limit(nbytes):
    return int(min(nbytes * 5 // 4 + (4 << 20), V7X_VMEM_BYTES - (4 << 20)))


def _params(sem, nbytes):
    return pltpu.CompilerParams(dimension_semantics=sem, vmem_limit_bytes=_vmem_limit(nbytes))


def _resident(shape):
    nd = len(shape)
    return pl.BlockSpec(shape, lambda *_: (0,) * nd, pipeline_mode=pl.Buffered(1))


def _layer_spec(w):
    stack, layer = w
    nd = stack.ndim
    return pl.BlockSpec((None,) + stack.shape[1:], lambda *_: (layer,) + (0,) * (nd - 1),
                        pipeline_mode=pl.Buffered(1))


def _layer_bytes(w):
    stack = w[0]
    return stack.size // stack.shape[0] * stack.dtype.itemsize


def _rows(tm, width):
    return pl.BlockSpec((tm, width), lambda i: (i, 0))


def _dot(a, b):
    return jnp.dot(a, b, preferred_element_type=F32)


def _dot_nt(a, b):
    return lax.dot_general(a, b, (((1,), (1,)), ((), ())), preferred_element_type=F32)


def _dot_tn(a, b):
    return lax.dot_general(a, b, (((0,), (0,)), ((), ())), preferred_element_type=F32)


def _rms(x, g):
    inv = lax.rsqrt(jnp.mean(x * x, axis=-1, keepdims=True) + EPS)
    return (x * inv) * g


def _log_sigmoid(x):
    return jnp.minimum(x, 0.0) - jnp.log1p(jnp.exp(-jnp.abs(x)))


def _scaled_dot(xg, w, inv):
    return _dot(xg, w) * jnp.concatenate([inv] * (w.shape[1] // V7X_LANES), axis=1)


def _norm_specs(d, tm):
    return [_rows(tm, d), _rows(tm, V7X_LANES)]


def _ffn_kernel(x_ref, g_ref, w_in_ref, w_out_ref, *rest, emit_norm):
    if emit_norm:
        gn_ref, xo_ref, xgo_ref, invo_ref, xg_s, act_s, ssq_s = rest
    else:
        xo_ref, xg_s, act_s = rest
    width = D_MODEL // FFN_PHASES
    ph = pl.program_id(1)

    def down_proj(k):
        act = act_s[...]
        ssq = jnp.zeros((x_ref.shape[0], 1), F32)
        for c in range(width // OUT_CHUNK):
            lo = k * width + c * OUT_CHUNK
            out_cols = slice(c * OUT_CHUNK, (c + 1) * OUT_CHUNK)
            xnew = x_ref[:, lo:lo + OUT_CHUNK] + 0.5 * _dot(act, w_out_ref[:, lo:lo + OUT_CHUNK])
            xo_ref[:, out_cols] = xnew
            if emit_norm:
                xgo_ref[:, out_cols] = (xnew * gn_ref[:, lo:lo + OUT_CHUNK]).astype(BF16)
                ssq = ssq + jnp.sum(xnew * xnew, axis=-1, keepdims=True)
        if emit_norm:
            if k > 0:
                ssq = ssq + ssq_s[:, 0:1]
            if k < FFN_PHASES - 1:
                ssq_s[...] = jnp.broadcast_to(ssq, ssq_s.shape)
            else:
                inv = lax.rsqrt(ssq * (1.0 / D_MODEL) + EPS)
                invo_ref[...] = jnp.broadcast_to(inv, invo_ref.shape)

    @pl.when(ph == 0)
    def _():
        x = x_ref[...]
        inv = lax.rsqrt(jnp.mean(x * x, axis=-1, keepdims=True) + EPS)
        xg_s[...] = (x * g_ref[...]).astype(BF16)
        xg = xg_s[...]
        for c in range(D_FF // FF_CHUNK):
            lo = c * FF_CHUNK
            gate = _dot(xg, w_in_ref[:, lo:lo + FF_CHUNK]) * inv
            up = _dot(xg, w_in_ref[:, D_FF + lo:D_FF + lo + FF_CHUNK]) * inv
            act_s[:, lo:lo + FF_CHUNK] = (gate * jax.nn.sigmoid(gate) * up).astype(BF16)
        down_proj(0)

    for k in range(1, FFN_PHASES):
        pl.when(ph == k)(functools.partial(down_proj, k))


def _ffn(x, g, w_in, w_out, g_next=None):
    n, d = x.shape
    width = d // FFN_PHASES
    emit_norm = g_next is not None
    out_shape = [jax.ShapeDtypeStruct((n, d), F32)]
    out_specs = [pl.BlockSpec((TM, width), lambda i, p: (i, p))]
    scratch = [pltpu.VMEM((TM, d), BF16), pltpu.VMEM((TM, D_FF), BF16)]
    operands = [x, g, w_in[0], w_out[0]]
    in_specs = [pl.BlockSpec((TM, d), lambda i, p: (i, 0)), _resident((1, d)), _layer_spec(w_in), _layer_spec(w_out)]
    if emit_norm:
        operands.append(g_next)
        in_specs.append(_resident((1, d)))
        out_shape += [jax.ShapeDtypeStruct((n, d), BF16), jax.ShapeDtypeStruct((n, V7X_LANES), F32)]
        out_specs += [pl.BlockSpec((TM, width), lambda i, p: (i, p)),
                      pl.BlockSpec((TM, V7X_LANES), lambda i, p: (i, 0))]
        scratch.append(pltpu.VMEM((TM, V7X_LANES), F32))
    nbytes = (_layer_bytes(w_in) + _layer_bytes(w_out) + 2 * TM * d * 4 + 2 * TM * width * (4 + 2)
              + TM * (d + D_FF) * 2)
    res = pl.pallas_call(
        functools.partial(_ffn_kernel, emit_norm=emit_norm),
        out_shape=out_shape,
        grid=(n // TM, FFN_PHASES),
        in_specs=in_specs,
        out_specs=out_specs,
        scratch_shapes=scratch,
        compiler_params=_params(("parallel", "arbitrary"), nbytes),
        name="ffn",
    )(*operands)
    return (res[0], (res[1], res[2])) if emit_norm else (res[0], None)


def _residual_out(x_ref, xo_ref, branch_fn):
    for c in range(D_MODEL // MIX_OUT_CHUNK):
        lo = c * MIX_OUT_CHUNK
        xo_ref[:, lo:lo + MIX_OUT_CHUNK] = x_ref[:, lo:lo + MIX_OUT_CHUNK] + branch_fn(lo, lo + MIX_OUT_CHUNK)


def _out_call(kernel, name, row_inputs, x, w_out, vectors):
    n, d = x.shape
    nbytes = _layer_bytes(w_out)
    nbytes += 2 * sum(TM * a.shape[1] * a.dtype.itemsize for a in row_inputs)
    nbytes += 2 * TM * d * (4 + 4) + 2 * TM * MIX_OUT_CHUNK * 4
    return pl.pallas_call(
        kernel,
        out_shape=jax.ShapeDtypeStruct((n, d), F32),
        grid=(n // TM,),
        in_specs=([_rows(TM, a.shape[1]) for a in row_inputs] + [_rows(TM, d)]
                  + [_layer_spec(w_out)] + [_resident(v.shape) for v in vectors]),
        out_specs=_rows(TM, d),
        compiler_params=_params(("parallel",), nbytes),
        name=name,
    )(*row_inputs, x, w_out[0], *vectors)


def _head_norm(p, gain):
    inv = lax.rsqrt(jnp.mean(p * p, axis=-1, keepdims=True) + EPS)
    return ((p * inv) * gain).astype(BF16)


def _even_in_kernel(xg_ref, inv_ref, w_ref, qg_ref, kg_ref, q_ref, k_ref, v_ref, glu_ref):
    xg = xg_ref[...]
    inv = inv_ref[...]
    hd = A_HEAD_DIM
    for c in range(A_Q // FF_CHUNK):
        p = _scaled_dot(xg, w_ref[:, c * FF_CHUNK:(c + 1) * FF_CHUNK], inv)
        for h in range(FF_CHUNK // hd):
            q_ref[:, c * FF_CHUNK + h * hd:c * FF_CHUNK + (h + 1) * hd] = _head_norm(
                p[:, h * hd:(h + 1) * hd], qg_ref[...])
    p = _scaled_dot(xg, w_ref[:, A_Q:A_Q + 2 * A_KV], inv)
    for h in range(A_KV_HEADS):
        k_ref[:, h * hd:(h + 1) * hd] = _head_norm(p[:, h * hd:(h + 1) * hd], kg_ref[...])
    v_ref[...] = p[:, A_KV:].astype(BF16)
    u0 = A_Q + 2 * A_KV
    for c in range(B_CH // FF_CHUNK):
        lo = c * FF_CHUNK
        u = _scaled_dot(xg, w_ref[:, u0 + lo:u0 + lo + FF_CHUNK], inv)
        gate = _scaled_dot(xg, w_ref[:, u0 + B_CH + lo:u0 + B_CH + lo + FF_CHUNK], inv)
        glu_ref[:, lo:lo + FF_CHUNK] = u * jax.nn.sigmoid(gate)


def _even_in(xn, w_in, q_gain, k_gain):
    n, d = xn[0].shape
    return pl.pallas_call(
        _even_in_kernel,
        out_shape=[jax.ShapeDtypeStruct((n, A_Q), BF16), jax.ShapeDtypeStruct((n, A_KV), BF16),
                   jax.ShapeDtypeStruct((n, A_KV), BF16), jax.ShapeDtypeStruct((n, B_CH), F32)],
        grid=(n // TM_IN,),
        in_specs=_norm_specs(d, TM_IN) + [_layer_spec(w_in), _resident((1, A_HEAD_DIM)),
                                          _resident((1, A_HEAD_DIM))],
        out_specs=[_rows(TM_IN, A_Q), _rows(TM_IN, A_KV), _rows(TM_IN, A_KV), _rows(TM_IN, B_CH)],
        compiler_params=_params(
            ("parallel",),
            _layer_bytes(w_in) + 2 * TM_IN * (d * 2 + A_Q * 2 + A_KV * 4 + B_CH * 4) + 4 * TM_IN * FF_CHUNK * 4),
        name="even_in",
    )(*xn, w_in[0], q_gain, k_gain)


def _attn_kernel(sink_ref, q_ref, k_ref, v_ref, o_ref, *, seq):
    blk, hd = A_BLOCK, A_HEAD_DIM
    span = 3 * blk
    scale = hd ** -0.5
    for sb in range(ATTN_BLOCKS):
        rows = slice(sb * blk, (sb + 1) * blk)
        n = pl.program_id(1) * ATTN_BLOCKS + sb
        start = pl.multiple_of(jnp.clip((n - 1) * blk, 0, seq - span), blk)
        kb = k_ref[0, pl.ds(start, span), :]
        vb = v_ref[0, pl.ds(start, span), :]
        qpos = n * blk + lax.broadcasted_iota(jnp.int32, (blk, span), 0)
        kpos = start + lax.broadcasted_iota(jnp.int32, (blk, span), 1)
        absrel = jnp.abs(kpos - qpos).astype(F32)
        valid = absrel <= float(A_WINDOW)
        for g in range(A_KV_HEADS):
            kg = kb[:, g * hd:(g + 1) * hd]
            vg = vb[:, g * hd:(g + 1) * hd]
            qg = jnp.concatenate(
                [q_ref[rows, (g * A_GROUP + j) * hd:(g * A_GROUP + j + 1) * hd] for j in range(A_GROUP)], axis=0)
            s = _dot_nt(qg, kg) * scale
            probs = []
            for j in range(A_GROUP):
                h = g * A_GROUP + j
                slope = 2.0 ** (-8.0 * (h + 1) / A_HEADS)
                sj = jnp.where(valid, s[j * blk:(j + 1) * blk] - slope * absrel, -jnp.inf)
                sink = sink_ref[h]
                m = jnp.maximum(jnp.max(sj, axis=-1, keepdims=True), sink)
                p = jnp.exp(sj - m)
                den = jnp.sum(p, axis=-1, keepdims=True) + jnp.exp(sink - m)
                probs.append((p * (1.0 / den)).astype(BF16))
            og = _dot(jnp.concatenate(probs, axis=0), vg)
            for j in range(A_GROUP):
                h = g * A_GROUP + j
                o_ref[rows, h * hd:(h + 1) * hd] = og[j * blk:(j + 1) * blk].astype(BF16)


def _attention(q, k, v, sink, bsz, seq):
    n = bsz * seq
    rows = ATTN_BLOCKS * A_BLOCK
    nb = seq // rows
    k3 = k.reshape(bsz, seq, A_KV)
    v3 = v.reshape(bsz, seq, A_KV)
    return pl.pallas_call(
        functools.partial(_attn_kernel, seq=seq),
        out_shape=jax.ShapeDtypeStruct((n, A_Q), BF16),
        grid=(bsz, nb),
        in_specs=[pl.BlockSpec(memory_space=pltpu.SMEM),
                  pl.BlockSpec((rows, A_Q), lambda b, i: (b * nb + i, 0)),
                  pl.BlockSpec((1, seq, A_KV), lambda b, i: (b, 0, 0)),
                  pl.BlockSpec((1, seq, A_KV), lambda b, i: (b, 0, 0))],
        out_specs=pl.BlockSpec((rows, A_Q), lambda b, i: (b * nb + i, 0)),
        compiler_params=_params(("parallel", "arbitrary"),
                                4 * seq * A_KV * 2 + 4 * rows * A_Q * 2 + 16 * A_BLOCK * 3 * A_BLOCK * 4),
        name="window_attn",
    )(sink, q, k3, v3)


def _halo_specs(width, bsz, seq, tile, halo, reverse):
    nt = seq // tile
    per_tile = tile // halo
    per_seq = seq // halo

    def t_of(i):
        return (nt - 1 - i) if reverse else i

    def prev_map(b, i):
        t = t_of(i)
        return (b * per_seq + jnp.maximum(t * per_tile - 1, 0), 0)

    def main_map(b, i):
        return (b * nt + t_of(i), 0)

    def next_map(b, i):
        t = t_of(i)
        return (b * per_seq + jnp.minimum((t + 1) * per_tile, per_seq - 1), 0)

    return [pl.BlockSpec((halo, width), prev_map), pl.BlockSpec((tile, width), main_map),
            pl.BlockSpec((halo, width), next_map)]


def _fill_padded_blocks(xp_ref, prev_ref, main_ref, next_ref, has_prev, has_next):
    halo = prev_ref.shape[0]
    tile = main_ref.shape[0]
    for s in range(xp_ref.shape[0]):
        cols = slice(s * V7X_LANES, (s + 1) * V7X_LANES)
        xp_ref[s, 0:halo, :] = jnp.where(has_prev, prev_ref[:, cols], 0.0)
        xp_ref[s, halo:halo + tile, :] = main_ref[:, cols]
        xp_ref[s, halo + tile:halo + tile + halo, :] = jnp.where(has_next, next_ref[:, cols], 0.0)


def _convb_kernel(prev_ref, main_ref, next_ref, w_ref, b_ref, g_ref, beta_ref, o_ref, xp_ref, y_ref):
    i = pl.program_id(1)
    _fill_padded_blocks(xp_ref, prev_ref, main_ref, next_ref, i > 0, i < pl.num_programs(1) - 1)
    first = CONV_HALO - (B_CONV - 1) // 2
    nrows = CONV_RC // CONV_STRIDE
    groups = nrows // V7X_SUBLANES
    phases = [(r0, p) for r0 in range(0, CONV_T, CONV_RC) for p in range(CONV_STRIDE)]
    for s in range(B_CH // V7X_LANES):
        cols = slice(s * V7X_LANES, (s + 1) * V7X_LANES)
        accs = [None] * len(phases)
        for j in range(B_CONV):
            wj = jnp.broadcast_to(w_ref[j:j + 1, cols], (V7X_SUBLANES, V7X_LANES))[None]
            for idx, (r0, p) in enumerate(phases):
                x = xp_ref[s, pl.ds(r0 + p + first + j, nrows, stride=CONV_STRIDE), :]
                term = x.reshape(groups, V7X_SUBLANES, V7X_LANES) * wj
                accs[idx] = term if accs[idx] is None else accs[idx] + term
        for idx, (r0, p) in enumerate(phases):
            y_ref[s, pl.ds(r0 + p, nrows, stride=CONV_STRIDE), :] = (
                accs[idx].reshape(nrows, V7X_LANES) + b_ref[:, cols])
    y = jnp.concatenate([y_ref[s] for s in range(B_CH // V7X_LANES)], axis=1)
    mu = jnp.mean(y, axis=-1, keepdims=True)
    yc = y - mu
    var = jnp.mean(yc * yc, axis=-1, keepdims=True)
    yn = (yc * lax.rsqrt(var + EPS)) * g_ref[...] + beta_ref[...]
    o_ref[...] = (yn * jax.nn.sigmoid(yn)).astype(BF16)


def _conv_b(glu, conv_w, conv_b, norm_g, norm_b, bsz, seq):
    n = bsz * seq
    nt = seq // CONV_T
    vec = _resident((1, B_CH))
    return pl.pallas_call(
        _convb_kernel,
        out_shape=jax.ShapeDtypeStruct((n, B_CH), BF16),
        grid=(bsz, nt),
        in_specs=_halo_specs(B_CH, bsz, seq, CONV_T, CONV_HALO, False) + [_resident((B_CONV, B_CH)), vec, vec, vec],
        out_specs=pl.BlockSpec((CONV_T, B_CH), lambda b, i: (b * nt + i, 0)),
        scratch_shapes=[pltpu.VMEM((B_CH // V7X_LANES, CONV_T + 2 * CONV_HALO, V7X_LANES), F32),
                        pltpu.VMEM((B_CH // V7X_LANES, CONV_T, V7X_LANES), F32)],
        compiler_params=_params(("parallel", "arbitrary"), 8 * CONV_T * B_CH * 4),
        name="conformer_conv",
    )(glu, glu, glu, conv_w, conv_b, norm_g, norm_b)


def _even_out_kernel(a_ref, c_ref, x_ref, w_ref, xo_ref):
    a = a_ref[...]
    c = c_ref[...]
    _residual_out(x_ref, xo_ref,
                  lambda lo, hi: _dot(a, w_ref[0:A_Q, lo:hi]) + _dot(c, w_ref[A_Q:A_Q + B_CH, lo:hi]))


def _even_out(a, c, x, w_out):
    return _out_call(_even_out_kernel, "even_out", [a, c], x, w_out, [])


def _odd_in_c_kernel(xg_ref, inv_ref, w_ref, gup_ref, gb_ref, q_ref, k_ref, v_ref, og_ref, laf_ref, lab_ref):
    xg = xg_ref[...]
    inv = inv_ref[...]
    o = 2 * C_QK + 2 * C_V
    lr = _scaled_dot(xg, w_ref[:, o:o + LR_PAD], inv).astype(BF16)
    pre = _dot(lr, gup_ref[...]) + gb_ref[...]
    loga = _log_sigmoid(pre) * (1.0 / C_TAU)
    laf_ref[...] = loga[:, 0:C_QK]
    lab_ref[...] = loga[:, C_QK:2 * C_QK]
    q_ref[...] = _scaled_dot(xg, w_ref[:, 0:C_QK], inv) * (C_DK ** -0.5)
    k_ref[...] = _scaled_dot(xg, w_ref[:, C_QK:2 * C_QK], inv)
    o = 2 * C_QK
    for c in range(C_V // FF_CHUNK):
        lo = c * FF_CHUNK
        v_ref[:, lo:lo + FF_CHUNK] = _scaled_dot(xg, w_ref[:, o + lo:o + lo + FF_CHUNK], inv).astype(BF16)
        og_ref[:, lo:lo + FF_CHUNK] = _scaled_dot(
            xg, w_ref[:, o + C_V + lo:o + C_V + lo + FF_CHUNK], inv).astype(BF16)


def _odd_in_c(xn, w_c, gup, gbias):
    n, d = xn[0].shape
    qk = jax.ShapeDtypeStruct((n, C_QK), F32)
    vv = jax.ShapeDtypeStruct((n, C_V), BF16)
    return pl.pallas_call(
        _odd_in_c_kernel,
        out_shape=[qk, qk, vv, vv, qk, qk],
        grid=(n // TM_IN,),
        in_specs=_norm_specs(d, TM_IN) + [_resident(w_c.shape), _resident(gup.shape), _resident(gbias.shape)],
        out_specs=[_rows(TM_IN, C_QK), _rows(TM_IN, C_QK), _rows(TM_IN, C_V), _rows(TM_IN, C_V),
                   _rows(TM_IN, C_QK), _rows(TM_IN, C_QK)],
        compiler_params=_params(
            ("parallel",),
            w_c.size * 2 + 2 * TM_IN * (d * 2 + 4 * C_QK * 4 + 2 * C_V * 2) + 6 * TM_IN * FF_CHUNK * 4),
        name="odd_in_c",
    )(*xn, w_c, gup, gbias)


def _odd_in_d_kernel(xg_ref, inv_ref, w_ref, xr_ref, yg_ref):
    xg = xg_ref[...]
    inv = inv_ref[...]
    for c in range(D_WIDTH // FF_CHUNK):
        lo = c * FF_CHUNK
        xr_ref[:, lo:lo + FF_CHUNK] = _scaled_dot(xg, w_ref[:, lo:lo + FF_CHUNK], inv)
        yg_ref[:, lo:lo + FF_CHUNK] = _scaled_dot(
            xg, w_ref[:, D_WIDTH + lo:D_WIDTH + lo + FF_CHUNK], inv).astype(BF16)


def _odd_in_d(xn, w_d):
    n, d = xn[0].shape
    return pl.pallas_call(
        _odd_in_d_kernel,
        out_shape=[jax.ShapeDtypeStruct((n, D_WIDTH), F32), jax.ShapeDtypeStruct((n, D_WIDTH), BF16)],
        grid=(n // TM_IN,),
        in_specs=_norm_specs(d, TM_IN) + [_resident(w_d.shape)],
        out_specs=[_rows(TM_IN, D_WIDTH), _rows(TM_IN, D_WIDTH)],
        compiler_params=_params(("parallel",),
                                w_d.size * 2 + 2 * TM_IN * (d * 2 + D_WIDTH * 6) + 4 * TM_IN * FF_CHUNK * 4),
        name="odd_in_d",
    )(*xn, w_d)


def _split3(x):
    hi = x.astype(BF16)
    r1 = x - hi.astype(F32)
    mid = r1.astype(BF16)
    lo = (r1 - mid.astype(F32)).astype(BF16)
    return hi, mid, lo


def _gla_prep(q, k, b, reverse):
    cc, sub = GLA_C, GLA_SUB
    nsub = cc // sub
    r128 = lax.broadcasted_iota(jnp.int32, (cc, C_DK), 0)
    zero_row = jnp.zeros((1, C_DK), F32)
    if reverse:
        refs = [b[(i + 1) * sub:(i + 1) * sub + 1] for i in range(nsub - 1)] + [zero_row]
        b_end = b[0:1]
    else:
        refs = [zero_row] + [b[i * sub - 1:i * sub] for i in range(1, nsub)]
        b_end = b[cc - 1:cc]
    ref_rows = jnp.concatenate([jnp.broadcast_to(r, (sub, C_DK)) for r in refs], axis=0)
    qt = q * jnp.exp(b - ref_rows)
    q_inter = (qt * jnp.exp(ref_rows)).astype(BF16)
    kvars = []
    for i in range(nsub):
        seen = (r128 >= i * sub) if reverse else (r128 < (i + 1) * sub)
        kvars.append((k * jnp.exp(jnp.where(seen, refs[i] - b, -1e30))).astype(BF16))
    k_out = (k * jnp.exp(b_end - b)).astype(BF16)
    return qt.astype(BF16), q_inter, jnp.concatenate(kvars, axis=0), k_out, jnp.exp(b_end)


def _gla_kernel(qf_ref, kf_ref, vf_ref, laf_ref, qb_ref, kb_ref, vb_ref, lab_ref, of_ref, ob_ref, sf_ref, sb_ref):
    @pl.when(pl.program_id(1) == 0)
    def _():
        sf_ref[...] = jnp.zeros_like(sf_ref)
        sb_ref[...] = jnp.zeros_like(sb_ref)

    cc, sub, tt = GLA_C, GLA_SUB, GLA_T
    nch = tt // cc
    dirs = ((qf_ref, kf_ref, vf_ref, laf_ref, of_ref, sf_ref, False),
            (qb_ref, kb_ref, vb_ref, lab_ref, ob_ref, sb_ref, True))
    units = [(d, c, h) for d in range(2) for c in range(nch) for h in range(C_HEADS)]

    row = lax.broadcasted_iota(jnp.int32, (tt, tt), 0)
    col = lax.broadcasted_iota(jnp.int32, (tt, tt), 1)
    same_chunk = (row // cc) == (col // cc)
    b_tile = []
    for (_, _, _, la_ref, _, _, reverse) in dirs:
        tri = (same_chunk & ((col >= row) if reverse else (col <= row))).astype(BF16)
        hi, mid, lo = _split3(la_ref[...])
        b_tile.append(_dot(tri, hi) + _dot(tri, mid) + _dot(tri, lo))

    prep = {}
    for (d, c, h) in units:
        q_ref, k_ref, _, _, _, _, reverse = dirs[d]
        rows = slice(c * cc, (c + 1) * cc)
        cols = slice(h * C_DK, (h + 1) * C_DK)
        prep[d, c, h] = _gla_prep(q_ref[rows, cols], k_ref[rows, cols], b_tile[d][rows, cols], reverse)

    scores = {u: _dot_nt(prep[u][0], prep[u][2]) for u in units}

    trow = lax.broadcasted_iota(jnp.int32, (cc, V7X_LANES), 0)
    lane = lax.broadcasted_iota(jnp.int32, (cc, V7X_LANES), 1)
    own_half = (lane // cc) == ((trow // sub) % 2)
    first_pair = (trow // sub) < 2
    att = {}
    for u in units:
        reverse = dirs[u[0]][6]
        key = lane % cc
        keep = own_half & ((key >= trow) if reverse else (key <= trow))
        r = scores[u]
        att[u] = jnp.where(keep, jnp.where(first_pair, r[:, 0:V7X_LANES], r[:, V7X_LANES:2 * V7X_LANES]), 0.0).astype(BF16)

    o_intra, kv = {}, {}
    for (d, c, h) in units:
        v = dirs[d][2][c * cc:(c + 1) * cc, h * C_DV:(h + 1) * C_DV]
        o_intra[d, c, h] = _dot(att[d, c, h], jnp.concatenate([v, v], axis=0))
        kv[d, c, h] = _dot_tn(v, prep[d, c, h][3])

    state = {(d, h): dirs[d][5][h] for d in range(2) for h in range(C_HEADS)}
    for step in range(nch):
        for d in range(2):
            c = (nch - 1 - step) if dirs[d][6] else step
            outs = []
            for h in range(C_HEADS):
                st = state[d, h]
                outs.append(o_intra[d, c, h] + _dot_nt(prep[d, c, h][1], st.astype(BF16)))
                state[d, h] = prep[d, c, h][4] * st + kv[d, c, h]
            dirs[d][4][c * cc:(c + 1) * cc, :] = jnp.concatenate(outs, axis=1).astype(BF16)
    for d in range(2):
        for h in range(C_HEADS):
            dirs[d][5][h] = state[d, h]


def _gla(q, k, v, la_f, la_b, bsz, seq):
    n = bsz * seq
    nt = seq // GLA_T

    def fwd(width):
        return pl.BlockSpec((GLA_T, width), lambda b, i: (b * nt + i, 0))

    def bwd(width):
        return pl.BlockSpec((GLA_T, width), lambda b, i: (b * nt + nt - 1 - i, 0))

    o = jax.ShapeDtypeStruct((n, C_V), BF16)
    st = pltpu.VMEM((C_HEADS, C_DV, C_DK), F32)
    return pl.pallas_call(
        _gla_kernel,
        out_shape=[o, o],
        grid=(bsz, nt),
        in_specs=[fwd(C_QK), fwd(C_QK), fwd(C_V), fwd(C_QK), bwd(C_QK), bwd(C_QK), bwd(C_V), bwd(C_QK)],
        out_specs=[fwd(C_V), bwd(C_V)],
        scratch_shapes=[st, st],
        compiler_params=_params(("parallel", "arbitrary"), 4 * GLA_T * (3 * C_QK * 4 + 2 * C_V * 2) + (8 << 20)),
        name="gla",
    )(q, k, v, la_f, q, k, v, la_b)


def _sublane_scan(a, u, reverse):
    t = a.shape[0]
    r8 = lax.broadcasted_iota(jnp.int32, a.shape, 0) & (V7X_SUBLANES - 1)
    d = 1
    while d < V7X_SUBLANES:
        if reverse:
            ush, ash, ok = pltpu.roll(u, t - d, 0), pltpu.roll(a, t - d, 0), r8 < V7X_SUBLANES - d
        else:
            ush, ash, ok = pltpu.roll(u, d, 0), pltpu.roll(a, d, 0), r8 >= d
        u = u + a * jnp.where(ok, ush, 0.0)
        a = a * jnp.where(ok, ash, 1.0)
        d *= 2
    return a, u


def _rglru_dir(xp_ref, hs_ref, cw_ref, cb_ref, w_ref, ba_ref, bx_ref, lam_ref, o_ref, carry_ref, reverse):
    seg = RG_SEG
    nm = CONV_T // seg
    nv = nm // V7X_SUBLANES
    first = V7X_SUBLANES - (D_CONV - 1) // 2
    r8 = lax.broadcasted_iota(jnp.int32, (V7X_SUBLANES, D_BLOCK_DIM), 0)
    tiny = float(jnp.finfo(F32).tiny)
    for blk in range(D_BLOCKS):
        cols = slice(blk * D_BLOCK_DIM, (blk + 1) * D_BLOCK_DIM)
        xcs = []
        for k in range(seg):
            xc = cb_ref[:, cols]
            for j in range(D_CONV):
                xc = xc + xp_ref[blk, pl.ds(first + k + j, nm, stride=seg), :] * cw_ref[j:j + 1, cols]
            xcs.append(xc)
        xc = jnp.concatenate(xcs, axis=0)
        pr = _dot(xc.astype(BF16), w_ref[blk])
        r = jax.nn.sigmoid(pr[:, 0:D_BLOCK_DIM] + ba_ref[:, cols])
        gate_i = jax.nn.sigmoid(pr[:, D_BLOCK_DIM:] + bx_ref[:, cols])
        log_a = (D_C * r) * _log_sigmoid(lam_ref[:, cols])
        a = jnp.exp(log_a)
        th = jnp.tanh(log_a)
        y = -2.0 * th / (1.0 - th)
        u = (y * lax.rsqrt(jnp.maximum(y, tiny))) * (gate_i * xc)
        ak = [a[k * nm:(k + 1) * nm] for k in range(seg)]
        uk = [u[k * nm:(k + 1) * nm] for k in range(seg)]
        for k in (range(seg - 2, -1, -1) if reverse else range(1, seg)):
            kp = k + 1 if reverse else k - 1
            uk[k] = uk[k] + ak[k] * uk[kp]
            ak[k] = ak[k] * ak[kp]
        edge = 0 if reverse else seg - 1
        seg_a, seg_u = _sublane_scan(ak[edge], uk[edge], reverse)
        c = carry_ref[:, cols]
        cins = [None] * nv
        for g in (range(nv - 1, -1, -1) if reverse else range(nv)):
            rows = slice(g * V7X_SUBLANES, (g + 1) * V7X_SUBLANES)
            hend = seg_u[rows] + seg_a[rows] * c
            if reverse:
                cins[g] = jnp.where(r8 == V7X_SUBLANES - 1, c, pltpu.roll(hend, V7X_SUBLANES - 1, 0))
                c = hend[0:1]
            else:
                cins[g] = jnp.where(r8 == 0, c, pltpu.roll(hend, 1, 0))
                c = hend[V7X_SUBLANES - 1:V7X_SUBLANES]
        carry_ref[:, cols] = c
        cin = jnp.concatenate(cins, axis=0)
        for k in range(seg):
            hs_ref[blk, pl.ds(k, nm, stride=seg), :] = uk[k] + ak[k] * cin
        o_ref[:, cols] = hs_ref[blk].astype(BF16)


def _rglru_kernel(pf_ref, mf_ref, nf_ref, pb_ref, mb_ref, nb_ref, cw_ref, cb_ref, wf_ref, wb_ref,
                  ba_ref, bx_ref, lam_ref, hf_ref, hb_ref, xpf_ref, xpb_ref, hsf_ref, hsb_ref, cf_ref, cr_ref):
    i = pl.program_id(1)
    last = pl.num_programs(1) - 1

    @pl.when(i == 0)
    def _():
        cf_ref[...] = jnp.zeros_like(cf_ref)
        cr_ref[...] = jnp.zeros_like(cr_ref)

    _fill_padded_blocks(xpf_ref, pf_ref, mf_ref, nf_ref, i > 0, i < last)
    _fill_padded_blocks(xpb_ref, pb_ref, mb_ref, nb_ref, i < last, i > 0)
    _rglru_dir(xpf_ref, hsf_ref, cw_ref, cb_ref, wf_ref, ba_ref.at[0], bx_ref.at[0], lam_ref.at[0], hf_ref, cf_ref, False)
    _rglru_dir(xpb_ref, hsb_ref, cw_ref, cb_ref, wb_ref, ba_ref.at[1], bx_ref.at[1], lam_ref.at[1], hb_ref, cr_ref, True)


def _rglru(xr, conv_w, conv_b, w_f, w_b, ba, bx, lam, bsz, seq):
    n = bsz * seq
    nt = seq // CONV_T
    o = jax.ShapeDtypeStruct((n, D_WIDTH), BF16)
    vec2 = _resident((2, 1, D_WIDTH))
    carry = pltpu.VMEM((1, D_WIDTH), F32)
    padded = pltpu.VMEM((D_BLOCKS, CONV_T + 2 * V7X_SUBLANES, D_BLOCK_DIM), F32)
    slabs = pltpu.VMEM((D_BLOCKS, CONV_T, D_BLOCK_DIM), F32)
    return pl.pallas_call(
        _rglru_kernel,
        out_shape=[o, o],
        grid=(bsz, nt),
        in_specs=(_halo_specs(D_WIDTH, bsz, seq, CONV_T, V7X_SUBLANES, False)
                  + _halo_specs(D_WIDTH, bsz, seq, CONV_T, V7X_SUBLANES, True)
                  + [_resident((D_CONV, D_WIDTH)), _resident((1, D_WIDTH)), _resident(w_f.shape),
                     _resident(w_b.shape), vec2, vec2, vec2]),
        out_specs=[pl.BlockSpec((CONV_T, D_WIDTH), lambda b, i: (b * nt + i, 0)),
                   pl.BlockSpec((CONV_T, D_WIDTH), lambda b, i: (b * nt + nt - 1 - i, 0))],
        scratch_shapes=[padded, padded, slabs, slabs, carry, carry],
        compiler_params=_params(("parallel", "arbitrary"), 12 * CONV_T * D_WIDTH * 4 + (8 << 20)),
        name="rglru",
    )(xr, xr, xr, xr, xr, xr, conv_w, conv_b, w_f, w_b, ba, bx, lam)


def _odd_out_kernel(of_ref, ob_ref, og_ref, hf_ref, hb_ref, yg_ref, x_ref, w_ref, cg_ref, xo_ref):
    c_parts = []
    for h in range(C_HEADS):
        cols = slice(h * C_DV, (h + 1) * C_DV)
        o = of_ref[:, cols].astype(F32) + ob_ref[:, cols].astype(F32)
        og = og_ref[:, cols].astype(F32)
        c_parts.append((_rms(o, cg_ref[...]) * (og * jax.nn.sigmoid(og))).astype(BF16))
    c_out = jnp.concatenate(c_parts, axis=1)
    hsum = hf_ref[...].astype(F32) + hb_ref[...].astype(F32)
    d_out = (hsum * jax.nn.gelu(yg_ref[...].astype(F32))).astype(BF16)
    _residual_out(x_ref, xo_ref,
                  lambda lo, hi: _dot(c_out, w_ref[0:C_V, lo:hi]) + _dot(d_out, w_ref[C_V:C_V + D_WIDTH, lo:hi]))


def _odd_out(o_f, o_b, og, h_f, h_b, yg, x, w_out, c_norm_g):
    return _out_call(_odd_out_kernel, "odd_out", [o_f, o_b, og, h_f, h_b, yg], x, w_out, [c_norm_g])


def _prepare(p):
    row = lambda v: v.reshape(1, -1).astype(F32)
    stacks = {name: p[name].astype(BF16) for name in
              ("ffn1_w_in", "ffn1_w_out", "ffn2_w_in", "ffn2_w_out", "ev_w_in", "ev_w_out", "od_w_out")}
    layers = []
    for layer in range(DEPTH):
        j = layer // 2
        lw = {
            "ln_ffn1": row(p["ln_ffn1"][layer]), "ln_mix": row(p["ln_mix"][layer]), "ln_ffn2": row(p["ln_ffn2"][layer]),
            "ffn1_w_in": (stacks["ffn1_w_in"], layer), "ffn1_w_out": (stacks["ffn1_w_out"], layer),
            "ffn2_w_in": (stacks["ffn2_w_in"], layer), "ffn2_w_out": (stacks["ffn2_w_out"], layer),
        }
        if layer % 2 == 0:
            lw.update({
                "w_in": (stacks["ev_w_in"], j), "w_out": (stacks["ev_w_out"], j),
                "q_gain": row(p["a_q_gain"][j]), "k_gain": row(p["a_k_gain"][j]), "sink": p["a_sink"][j].astype(F32),
                "conv_w": p["b_conv_w"][j].astype(F32), "conv_b": row(p["b_conv_b"][j]),
                "norm_g": row(p["b_norm_g"][j]), "norm_b": row(p["b_norm_b"][j]),
            })
        else:
            w = p["od_w_in"][j]
            o4 = 2 * C_QK + 2 * C_V
            o5 = o4 + 2 * C_RANK
            w_c = jnp.concatenate([w[:, :o4], w[:, o4:o5], jnp.zeros((D_MODEL, LR_PAD - 2 * C_RANK), w.dtype)], axis=1)
            gup = jnp.zeros((LR_PAD, 2 * C_QK), F32)
            gup = gup.at[0:C_RANK, 0:C_QK].set(p["c_gate_up"][j, 0])
            gup = gup.at[C_RANK:2 * C_RANK, C_QK:2 * C_QK].set(p["c_gate_up"][j, 1])
            lw.update({
                "w_c": w_c.astype(BF16), "w_d": w[:, o5:].astype(BF16), "w_out": (stacks["od_w_out"], j),
                "gup": gup.astype(BF16), "gbias": p["c_gate_bias"][j].reshape(1, 2 * C_QK).astype(F32),
                "c_norm_g": row(p["c_norm_g"][j]),
                "conv_w": p["d_conv_w"][j].astype(F32), "conv_b": row(p["d_conv_b"][j]),
                "w_f": jnp.concatenate([p["d_wa"][j, 0], p["d_wx"][j, 0]], axis=-1).astype(BF16),
                "w_b": jnp.concatenate([p["d_wa"][j, 1], p["d_wx"][j, 1]], axis=-1).astype(BF16),
                "ba": p["d_ba"][j].reshape(2, 1, D_WIDTH).astype(F32), "bx": p["d_bx"][j].reshape(2, 1, D_WIDTH).astype(F32),
                "lam": p["d_lambda"][j].reshape(2, 1, D_WIDTH).astype(F32),
            })
        layers.append(lw)
    return layers


def _run_trunk(x3, layers):
    bsz, seq, d = x3.shape
    x = x3.reshape(bsz * seq, d)
    for layer, lw in enumerate(layers):
        x, xn = _ffn(x, lw["ln_ffn1"], lw["ffn1_w_in"], lw["ffn1_w_out"], lw["ln_mix"])
        if layer % 2 == 0:
            q, k, v, glu = _even_in(xn, lw["w_in"], lw["q_gain"], lw["k_gain"])
            a = _attention(q, k, v, lw["sink"], bsz, seq)
            c = _conv_b(glu, lw["conv_w"], lw["conv_b"], lw["norm_g"], lw["norm_b"], bsz, seq)
            x = _even_out(a, c, x, lw["w_out"])
        else:
            q, k, v, og, la_f, la_b = _odd_in_c(xn, lw["w_c"], lw["gup"], lw["gbias"])
            xr, yg = _odd_in_d(xn, lw["w_d"])
            o_f, o_b = _gla(q, k, v, la_f, la_b, bsz, seq)
            h_f, h_b = _rglru(xr, lw["conv_w"], lw["conv_b"], lw["w_f"], lw["w_b"], lw["ba"], lw["bx"], lw["lam"],
                              bsz, seq)
            x = _odd_out(o_f, o_b, og, h_f, h_b, yg, x, lw["w_out"], lw["c_norm_g"])
        x, _ = _ffn(x, lw["ln_ffn2"], lw["ffn2_w_in"], lw["ffn2_w_out"])
    return x.reshape(bsz, seq, d)


def kernel(x_prompt, x_sample, ln_ffn1, ffn1_w_in, ffn1_w_out, ln_mix, ln_ffn2, ffn2_w_in, ffn2_w_out, ev_w_in, ev_w_out, a_q_gain, a_k_gain, a_sink, b_conv_w, b_conv_b, b_norm_g, b_norm_b, od_w_in, od_w_out, c_gate_up, c_gate_bias, c_norm_g, d_conv_w, d_conv_b, d_wa, d_ba, d_wx, d_bx, d_lambda):
    layers = _prepare(dict(
        ln_ffn1=ln_ffn1, ffn1_w_in=ffn1_w_in, ffn1_w_out=ffn1_w_out, ln_mix=ln_mix, ln_ffn2=ln_ffn2,
        ffn2_w_in=ffn2_w_in, ffn2_w_out=ffn2_w_out, ev_w_in=ev_w_in, ev_w_out=ev_w_out, a_q_gain=a_q_gain,
        a_k_gain=a_k_gain, a_sink=a_sink, b_conv_w=b_conv_w, b_conv_b=b_conv_b, b_norm_g=b_norm_g,
        b_norm_b=b_norm_b, od_w_in=od_w_in, od_w_out=od_w_out, c_gate_up=c_gate_up, c_gate_bias=c_gate_bias,
        c_norm_g=c_norm_g, d_conv_w=d_conv_w, d_conv_b=d_conv_b, d_wa=d_wa, d_ba=d_ba, d_wx=d_wx, d_bx=d_bx,
        d_lambda=d_lambda))
    return (_run_trunk(x_prompt, layers), _run_trunk(x_sample, layers))
```
